```python
import jax, jax.numpy as jnp
from jax import lax
import numpy as np

D_MODEL = 1024
BATCH = 16
SEQ = 2048
DEPTH = 2
DEC_BATCH = 32
DEC_SEQ = 8
PAST_LEN = 16384
PAGE_SIZE = 128

HEAD_DIM = 64
H_SB = D_MODEL // (2 * HEAD_DIM)
H_FOX = D_MODEL // (2 * HEAD_DIM)
N_HEADS = H_SB + H_FOX
MIX_WIDTH = N_HEADS * HEAD_DIM
W_SB = H_SB * HEAD_DIM
W_FOX = H_FOX * HEAD_DIM
IN_COLS = 3 * MIX_WIDTH + W_FOX + H_FOX
N_META = 16
Q_BLOCK = 128
N_GROUPS = 4
EXPERTS_PER_GROUP = 8
N_EXPERTS = N_GROUPS * EXPERTS_PER_GROUP
TOP_K_IN_GROUP = 2
D_EXPERT = D_MODEL // 2
MOE_BLOCK = 256
RMS_EPS = 1e-6
FORGET_BIAS_MIN = 2.0
FORGET_BIAS_MAX = 10.0

kernel_name = 'stick_fox_hymba_hmoe_step'


def _rmsnorm(x, g):
    xf = x.astype(jnp.float32)
    y = xf * lax.rsqrt(jnp.mean(xf * xf, axis=-1, keepdims=True) + RMS_EPS)
    return y.astype(x.dtype) * g


def _mixer_inputs(x, g_norm, w_in, b_forget):
    h = _rmsnorm(x, g_norm)
    z = h @ w_in
    heads = z.shape[:-1] + (N_HEADS, HEAD_DIM)
    q = z[..., :MIX_WIDTH].reshape(heads)
    k = z[..., MIX_WIDTH:2 * MIX_WIDTH].reshape(heads)
    v = z[..., 2 * MIX_WIDTH:3 * MIX_WIDTH].reshape(heads)
    gate = z[..., 3 * MIX_WIDTH:3 * MIX_WIDTH + W_FOX]
    logf = jax.nn.log_sigmoid(z[..., 3 * MIX_WIDTH + W_FOX:].astype(jnp.float32) + b_forget)
    return q, k, v, gate, logf


def _sb_weights(z, valid, tail):
    l1m = jnp.where(valid, jax.nn.log_sigmoid(-z), 0.0)
    suffix = lax.cumsum(l1m, axis=z.ndim - 1, reverse=True) - l1m + tail
    w = jnp.where(valid, jnp.exp(jax.nn.log_sigmoid(z) + suffix), 0.0)
    return w, jnp.sum(l1m, axis=-1, keepdims=True)


def _mix_prompt(q, k, v, logf):
    B, T = q.shape[0], q.shape[1]
    Tp = -(-T // Q_BLOCK) * Q_BLOCK
    pad4 = ((0, 0), (0, Tp - T), (0, 0), (0, 0))
    q, k, v = jnp.pad(q, pad4), jnp.pad(k, pad4), jnp.pad(v, pad4)
    c = jnp.cumsum(jnp.pad(logf, ((0, 0), (0, Tp - T), (0, 0))), axis=1)
    c_k = jnp.transpose(c, (0, 2, 1))[:, :, None, :]
    kpos = jnp.arange(Tp)
    kvalid = kpos < T
    k_sb, v_sb, k_fx, v_fx = k[:, :, :H_SB], v[:, :, :H_SB], k[:, :, H_SB:], v[:, :, H_SB:]
    scale = HEAD_DIM ** -0.5

    def block(i):
        s = i * Q_BLOCK
        qb = lax.dynamic_slice_in_dim(q, s, Q_BLOCK, axis=1)
        qpos = s + jnp.arange(Q_BLOCK)
        strict = (kpos[None, :] < qpos[:, None]) & kvalid[None, :]
        incl = (kpos[None, :] <= qpos[:, None]) & kvalid[None, :]
        z_sb = jnp.einsum('bqhd,bkhd->bhqk', qb[:, :, :H_SB], k_sb, preferred_element_type=jnp.float32) * scale
        w_sb, _ = _sb_weights(z_sb, strict, 0.0)
        o_sb = jnp.einsum('bhqk,bkhd->bqhd', w_sb.astype(v.dtype), v_sb)
        c_q = jnp.transpose(lax.dynamic_slice_in_dim(c, s, Q_BLOCK, axis=1), (0, 2, 1))[..., None]
        z_fx = jnp.einsum('bqhd,bkhd->bhqk', qb[:, :, H_SB:], k_fx, preferred_element_type=jnp.float32) * scale + c_q - c_k
        p = jax.nn.softmax(jnp.where(incl, z_fx, -jnp.inf), axis=-1)
        o_fx = jnp.einsum('bhqk,bkhd->bqhd', p.astype(v.dtype), v_fx)
        return jnp.concatenate([o_sb, o_fx], axis=2)

    o = lax.map(block, jnp.arange(Tp // Q_BLOCK))
    o = jnp.moveaxis(o, 0, 1).reshape(B, Tp, N_HEADS, HEAD_DIM)
    return o[:, :T]


def _mix_sample(q, k, v, logf, pk_sb, pv_sb, pk_fx, pv_fx, plogf):
    S = q.shape[1]
    P = pk_sb.shape[1]
    scale = HEAD_DIM ** -0.5
    i = jnp.arange(S)
    strict = i[None, :] < i[:, None]
    incl = i[None, :] <= i[:, None]
    q_sb = q[:, :, :H_SB]
    zp = jnp.einsum('bqhd,bkhd->bhqk', q_sb, pk_sb, preferred_element_type=jnp.float32) * scale
    zn = jnp.einsum('bqhd,bkhd->bhqk', q_sb, k[:, :, :H_SB], preferred_element_type=jnp.float32) * scale
    wn, tail = _sb_weights(zn, strict, 0.0)
    wp, _ = _sb_weights(zp, True, tail)
    o_sb = (jnp.einsum('bhqk,bkhd->bqhd', wp.astype(v.dtype), pv_sb)
            + jnp.einsum('bhqk,bkhd->bqhd', wn.astype(v.dtype), v[:, :, :H_SB]))
    plf = plogf.astype(jnp.float32)
    rc = lax.cumsum(plf, axis=1, reverse=True) - plf
    cn = jnp.cumsum(logf, axis=1)
    c_q = jnp.transpose(cn, (0, 2, 1))[..., None]
    q_fx = q[:, :, H_SB:]
    lp = (jnp.einsum('bqhd,bkhd->bhqk', q_fx, pk_fx, preferred_element_type=jnp.float32) * scale
          + c_q + jnp.transpose(rc, (0, 2, 1))[:, :, None, :])
    ln = (jnp.einsum('bqhd,bkhd->bhqk', q_fx, k[:, :, H_SB:], preferred_element_type=jnp.float32) * scale
          + c_q - jnp.transpose(cn, (0, 2, 1))[:, :, None, :])
    ln = jnp.where(incl, ln, -jnp.inf)
    p = jax.nn.softmax(jnp.concatenate([lp, ln], axis=-1), axis=-1)
    o_fx = (jnp.einsum('bhqk,bkhd->bqhd', p[..., :P].astype(v.dtype), pv_fx)
            + jnp.einsum('bhqk,bkhd->bqhd', p[..., P:].astype(v.dtype), v[:, :, H_SB:]))
    return jnp.concatenate([o_sb, o_fx], axis=2)


def _merge(o, gate, g_sb, g_fox, w_out):
    lead = o.shape[:-2]
    o_sb = _rmsnorm(o[..., :H_SB, :].reshape(lead + (W_SB,)), g_sb)
    o_fx = _rmsnorm(o[..., H_SB:, :].reshape(lead + (W_FOX,)), g_fox) * jax.nn.sigmoid(gate)
    return jnp.concatenate([o_sb, o_fx], axis=-1) @ w_out


def _grouped_experts(h, eid, gate, w1, w3, w2):
    n_tok, k = eid.shape
    n_asg = n_tok * k
    n_exp = w1.shape[0]
    d = h.shape[1]
    flat_e = eid.reshape(n_asg)
    flat_t = jnp.repeat(jnp.arange(n_tok, dtype=jnp.int32), k)
    order = jnp.argsort(flat_e)
    se = flat_e[order]
    counts = jnp.bincount(flat_e, length=n_exp)
    padded = (counts + MOE_BLOCK - 1) // MOE_BLOCK * MOE_BLOCK
    ends_p = jnp.cumsum(padded)
    dest = (ends_p - padded)[se] + jnp.arange(n_asg) - (jnp.cumsum(counts) - counts)[se]
    n_blocks = -(-n_asg // MOE_BLOCK) + n_exp
    n_slots = n_blocks * MOE_BLOCK
    slot_tok = jnp.full((n_slots,), n_tok, jnp.int32).at[dest].set(flat_t[order])
    slot_gate = jnp.zeros((n_slots,), h.dtype).at[dest].set(gate.reshape(n_asg)[order])
    block_exp = jnp.minimum(jnp.searchsorted(ends_p, jnp.arange(n_blocks) * MOE_BLOCK, side='right'), n_exp - 1)
    hp = jnp.concatenate([h, jnp.zeros((1, d), h.dtype)], axis=0)
    xb = hp[slot_tok].reshape(n_blocks, MOE_BLOCK, d)

    def run(args):
        xblk, e = args
        return (jax.nn.silu(xblk @ w1[e]) * (xblk @ w3[e])) @ w2[e]

    yb = lax.map(run, (xb, block_exp)).reshape(n_slots, d)
    y = jnp.zeros_like(hp).at[slot_tok].add(yb * slot_gate[:, None])
    return y[:n_tok]


def _hier_moe(h, w_group, b_group, w_router, b_router, w1, w3, w2):
    hf = h.reshape(-1, D_MODEL)
    n = hf.shape[0]
    g_logits = (hf @ w_group).astype(jnp.float32) + b_group
    grp = jnp.argmax(g_logits, axis=-1).astype(jnp.int32)
    p_grp = jnp.take_along_axis(jax.nn.softmax(g_logits, axis=-1), grp[:, None], axis=1)
    e_logits = ((hf @ w_router).astype(jnp.float32) + b_router).reshape(n, N_GROUPS, EXPERTS_PER_GROUP)
    e_logits = jnp.take_along_axis(e_logits, grp[:, None, None], axis=1)[:, 0]
    top_v, top_i = lax.top_k(e_logits, TOP_K_IN_GROUP)
    gate = p_grp * jax.nn.softmax(top_v, axis=-1)
    eid = grp[:, None] * EXPERTS_PER_GROUP + top_i.astype(jnp.int32)
    y = _grouped_experts(hf, eid, gate.astype(hf.dtype), w1, w3, w2)
    return y.reshape(h.shape)


def setup_inputs(seed: int = 0) -> dict:
    key = jax.random.key(seed)
    ks = jax.random.split(key, 24)
    f32 = jnp.float32
    n_pages = PAST_LEN // PAGE_SIZE
    n_used = DEC_BATCH * n_pages
    n_pool = n_used + n_used // 4
    nrm = lambda k, s: jax.random.normal(k, s, f32)
    page_table = jax.random.permutation(ks[2], n_pool)[:n_used].reshape(DEC_BATCH, n_pages).astype(jnp.int32)
    head_bias = jnp.linspace(FORGET_BIAS_MIN, FORGET_BIAS_MAX, H_FOX, dtype=f32)
    return {
        'x_prompt': nrm(ks[0], (BATCH, SEQ, D_MODEL)),
        'x_sample': nrm(ks[1], (DEC_BATCH, DEC_SEQ, D_MODEL)),
        'cache_k': nrm(ks[3], (DEPTH, n_pool, PAGE_SIZE, N_HEADS, HEAD_DIM)),
        'cache_v': nrm(ks[4], (DEPTH, n_pool, PAGE_SIZE, N_HEADS, HEAD_DIM)),
        'cache_logf': jax.nn.log_sigmoid(head_bias + nrm(ks[5], (DEPTH, n_pool, PAGE_SIZE, H_FOX))),
        'page_table': page_table,
        'meta_tokens': nrm(ks[6], (N_META, D_MODEL)),
        'norm_mix_g': 1.0 + 0.02 * nrm(ks[7], (DEPTH, D_MODEL)),
        'w_in': nrm(ks[8], (DEPTH, D_MODEL, IN_COLS)) * D_MODEL ** -0.5,
        'b_forget': head_bias[None, :] + 0.1 * nrm(ks[9], (DEPTH, H_FOX)),
        'norm_sb_g': 1.0 + 0.02 * nrm(ks[10], (DEPTH, W_SB)),
        'norm_fox_g': 1.0 + 0.02 * nrm(ks[11], (DEPTH, W_FOX)),
        'w_out': nrm(ks[12], (DEPTH, MIX_WIDTH, D_MODEL)) * MIX_WIDTH ** -0.5,
        'norm_ffn_g': 1.0 + 0.02 * nrm(ks[13], (DEPTH, D_MODEL)),
        'w_group': nrm(ks[14], (DEPTH, D_MODEL, N_GROUPS)) * D_MODEL ** -0.5,
        'b_group': 0.01 * nrm(ks[15], (DEPTH, N_GROUPS)),
        'w_router': nrm(ks[16], (DEPTH, D_MODEL, N_EXPERTS)) * D_MODEL ** -0.5,
        'b_router': 0.01 * nrm(ks[17], (DEPTH, N_EXPERTS)),
        'w_exp_gate': nrm(ks[18], (DEPTH, N_EXPERTS, D_MODEL, D_EXPERT)) * D_MODEL ** -0.5,
        'w_exp_up': nrm(ks[19], (DEPTH, N_EXPERTS, D_MODEL, D_EXPERT)) * D_MODEL ** -0.5,
        'w_exp_down': nrm(ks[20], (DEPTH, N_EXPERTS, D_EXPERT, D_MODEL)) * D_EXPERT ** -0.5,
        'norm_final_g': 1.0 + 0.02 * nrm(ks[21], (D_MODEL,)),
    }


def reference(x_prompt, x_sample, cache_k, cache_v, cache_logf, page_table, meta_tokens,
              norm_mix_g, w_in, b_forget, norm_sb_g, norm_fox_g, w_out, norm_ffn_g,
              w_group, b_group, w_router, b_router, w_exp_gate, w_exp_up, w_exp_down, norm_final_g):
    B = x_prompt.shape[0]
    DB = x_sample.shape[0]
    meta = jnp.broadcast_to(meta_tokens[None].astype(x_prompt.dtype), (B, N_META, D_MODEL))
    xp = jnp.concatenate([meta, x_prompt], axis=1)
    xs = x_sample
    kp_rows, vp_rows, lp_rows, ks_rows, vs_rows, ls_rows = [], [], [], [], [], []
    for l in range(DEPTH):
        q, k, v, gate, logf = _mixer_inputs(xp, norm_mix_g[l], w_in[l], b_forget[l])
        o = _mix_prompt(q, k, v, logf)
        xp = xp + _merge(o, gate, norm_sb_g[l], norm_fox_g[l], w_out[l])
        xp = xp + _hier_moe(_rmsnorm(xp, norm_ffn_g[l]), w_group[l], b_group[l], w_router[l], b_router[l],
                            w_exp_gate[l], w_exp_up[l], w_exp_down[l])
        kp_rows.append(k)
        vp_rows.append(v)
        lp_rows.append(logf)
        q, k, v, gate, logf = _mixer_inputs(xs, norm_mix_g[l], w_in[l], b_forget[l])
        pk_sb = cache_k[l, page_table, :, :H_SB].reshape(DB, -1, H_SB, HEAD_DIM)
        pv_sb = cache_v[l, page_table, :, :H_SB].reshape(DB, -1, H_SB, HEAD_DIM)
        pk_fx = cache_k[l, page_table, :, H_SB:].reshape(DB, -1, H_FOX, HEAD_DIM)
        pv_fx = cache_v[l, page_table, :, H_SB:].reshape(DB, -1, H_FOX, HEAD_DIM)
        plogf = cache_logf[l, page_table].reshape(DB, -1, H_FOX)
        o = _mix_sample(q, k, v, logf, pk_sb, pv_sb, pk_fx, pv_fx, plogf)
        xs = xs + _merge(o, gate, norm_sb_g[l], norm_fox_g[l], w_out[l])
        xs = xs + _hier_moe(_rmsnorm(xs, norm_ffn_g[l]), w_group[l], b_group[l], w_router[l], b_router[l],
                            w_exp_gate[l], w_exp_up[l], w_exp_down[l])
        ks_rows.append(k)
        vs_rows.append(v)
        ls_rows.append(logf)
    y_prompt = _rmsnorm(xp, norm_final_g)[:, N_META:]
    y_sample = _rmsnorm(xs, norm_final_g)
    new_k_prompt = jnp.stack(kp_rows)
    new_v_prompt = jnp.stack(vp_rows)
    new_logf_prompt = jnp.stack(lp_rows)
    new_k_sample = jnp.stack(ks_rows)
    new_v_sample = jnp.stack(vs_rows)
    new_logf_sample = jnp.stack(ls_rows)
    return (y_prompt, y_sample, new_k_prompt, new_v_prompt, new_logf_prompt, new_k_sample, new_v_sample, new_logf_sample)
```

```python
import functools

import jax
import jax.numpy as jnp
from jax import lax
from jax.experimental import pallas as pl
from jax.experimental.pallas import tpu as pltpu

F32 = jnp.float32
BF16 = jnp.bfloat16

D_MODEL = 1024
HEAD_DIM = 64
H_SB = 8
H_FOX = 8
N_HEADS = H_SB + H_FOX
W_SB = H_SB * HEAD_DIM
W_FOX = H_FOX * HEAD_DIM
N_META = 16
N_GROUPS = 4
EXPERTS_PER_GROUP = 8
N_EXPERTS = N_GROUPS * EXPERTS_PER_GROUP
D_EXPERT = D_MODEL // 2
MOE_BLOCK = 256
RMS_EPS = 1e-6
PAGE_SIZE = 128
LANES = 128
ATTN_BLOCK = 256
PAGES_PER_STEP = 4
NEG_BIG = -1e30
VMEM_LIMIT = 56 * 1024 * 1024


def _cparams(*sem):
    return pltpu.CompilerParams(dimension_semantics=sem, vmem_limit_bytes=VMEM_LIMIT)


def _log_sigmoid(x):
    return jnp.minimum(x, 0.0) - jnp.log1p(jnp.exp(-jnp.abs(x)))


def _rms_scale(x):
    return x * lax.rsqrt(jnp.mean(x * x, axis=-1, keepdims=True) + RMS_EPS)


def _dot_nt(a, b):
    return lax.dot_general(a, b, (((1,), (1,)), ((), ())), preferred_element_type=F32)


def _suffix_matrix(n):
    j = lax.broadcasted_iota(jnp.int32, (n, n), 0)
    s = lax.broadcasted_iota(jnp.int32, (n, n), 1)
    return jnp.where(j > s, 1.0, 0.0).astype(BF16)


def _split_dot(x, m01, parts):
    acc = None
    rem = x
    for p in range(parts):
        piece = rem.astype(BF16)
        d = jnp.dot(piece, m01, preferred_element_type=F32)
        acc = d if acc is None else acc + d
        if p + 1 < parts:
            rem = rem - piece.astype(F32)
    return acc


def _inproj_body(n_res, *refs):
    x_ref = refs[0]
    res_refs = refs[1:1 + n_res]
    g_ref, w_ref, wf_ref, bf_ref = refs[1 + n_res:5 + n_res]
    outs = refs[5 + n_res:]
    x = x_ref[...]
    for r in res_refs:
        x = x + r[...]
    if n_res:
        xo_ref, outs = outs[0], outs[1:]
        xo_ref[...] = x
    q_ref, k_ref, v_ref, gate_ref, logf_ref = outs
    h = (_rms_scale(x) * g_ref[...]).astype(BF16)
    half = D_MODEL // 2
    scale = HEAD_DIM ** -0.5
    for c in range(7):
        z = jnp.dot(h, w_ref[:, c * half:(c + 1) * half], preferred_element_type=F32)
        dst = pl.ds((c % 2) * half, half)
        if c < 2:
            q_ref[:, dst] = (z * scale).astype(BF16)
        elif c < 4:
            k_ref[:, dst] = z
        elif c < 6:
            v_ref[:, dst] = z
        else:
            gate_ref[...] = z.astype(BF16)
    zf = jnp.dot(h, wf_ref[...], preferred_element_type=F32) + bf_ref[...]
    logf_ref[...] = _log_sigmoid(zf)


def _inproj(x, res, g, w_main, w_f, b_f, tm):
    n = x.shape[0]
    n_res = len(res)
    row = lambda c: pl.BlockSpec((tm, c), lambda i: (i, 0))
    full = lambda a: pl.BlockSpec(a.shape, lambda i: (0,) * a.ndim)
    out_shape = [jax.ShapeDtypeStruct((n, D_MODEL), BF16),
                 jax.ShapeDtypeStruct((n, D_MODEL), F32),
                 jax.ShapeDtypeStruct((n, D_MODEL), F32),
                 jax.ShapeDtypeStruct((n, W_FOX), BF16),
                 jax.ShapeDtypeStruct((n, H_FOX), F32)]
    out_specs = [row(D_MODEL), row(D_MODEL), row(D_MODEL), row(W_FOX), row(H_FOX)]
    if n_res:
        out_shape = [jax.ShapeDtypeStruct((n, D_MODEL), F32)] + out_shape
        out_specs = [row(D_MODEL)] + out_specs
    return pl.pallas_call(
        functools.partial(_inproj_body, n_res),
        grid=(n // tm,),
        in_specs=[row(D_MODEL)] * (1 + n_res) + [full(g), full(w_main), full(w_f), full(b_f)],
        out_specs=out_specs,
        out_shape=out_shape,
        compiler_params=_cparams("parallel"),
        name="inproj",
    )(x, *res, g, w_main, w_f, b_f)


def _stage_heads(q_ref, k_ref, v_ref, qs, ks, vs):
    for hh in range(2):
        lanes = slice(hh * HEAD_DIM, (hh + 1) * HEAD_DIM)
        qs[hh] = q_ref[0, :, lanes]
        ks[hh] = k_ref[0, :, lanes].astype(BF16)
        vs[hh] = v_ref[0, :, lanes].astype(BF16)


def _sb_tile(q, k, v, tail, diag):
    tq, tk = q.shape[0], k.shape[0]
    z = _dot_nt(q, k)
    ls = _log_sigmoid(z)
    l1m = ls - z
    if diag:
        row = lax.broadcasted_iota(jnp.int32, (tq, tk), 0)
        col = lax.broadcasted_iota(jnp.int32, (tq, tk), 1)
        valid = col < row
        l1m = jnp.where(valid, l1m, 0.0)
    suffix = _split_dot(l1m, _suffix_matrix(tk), 2)
    w = jnp.exp(ls + suffix + tail)
    if diag:
        w = jnp.where(valid, w, 0.0)
    o = jnp.dot(w.astype(BF16), v, preferred_element_type=F32)
    tail = tail + suffix[:, 0:1] + l1m[:, 0:1]
    return o, tail


def _sb_prompt_body(q_ref, k_ref, v_ref, o_ref, qs, ks, vs):
    _stage_heads(q_ref, k_ref, v_ref, qs, ks, vs)
    t = q_ref.shape[1]
    n_blocks = (t - N_META) // ATTN_BLOCK
    meta = pl.ds(0, N_META)
    for hh in range(2):
        lanes = slice(hh * HEAD_DIM, (hh + 1) * HEAD_DIM)
        o_meta, _ = _sb_tile(qs[hh, meta, :], ks[hh, meta, :], vs[hh, meta, :],
                             jnp.zeros((N_META, 1), F32), True)
        o_ref[0, meta, lanes] = o_meta

        def q_block(qi, carry, hh=hh, lanes=lanes):
            r0 = pl.multiple_of(N_META + qi * ATTN_BLOCK, 16)
            rows = pl.ds(r0, ATTN_BLOCK)
            q = qs[hh, rows, :]
            o, tail = _sb_tile(q, ks[hh, rows, :], vs[hh, rows, :],
                               jnp.zeros((ATTN_BLOCK, 1), F32), True)

            def k_block(step, c):
                o, tail = c
                k0 = pl.multiple_of(N_META + (qi - 1 - step) * ATTN_BLOCK, 16)
                keys = pl.ds(k0, ATTN_BLOCK)
                do, tail = _sb_tile(q, ks[hh, keys, :], vs[hh, keys, :], tail, False)
                return o + do, tail

            o, tail = lax.fori_loop(0, qi, k_block, (o, tail))
            do, _ = _sb_tile(q, ks[hh, meta, :], vs[hh, meta, :], tail, False)
            o_ref[0, rows, lanes] = o + do
            return carry

        lax.fori_loop(0, n_blocks, q_block, 0)


def _fx_tile(q, k, v, cq, ck, m, l, acc, diag):
    tq, tk = q.shape[0], k.shape[0]
    z = _dot_nt(q, k) + cq - ck
    if diag:
        row = lax.broadcasted_iota(jnp.int32, (tq, tk), 0)
        col = lax.broadcasted_iota(jnp.int32, (tq, tk), 1)
        z = jnp.where(col <= row, z, -jnp.inf)
    m_new = jnp.maximum(m, jnp.max(z, axis=-1, keepdims=True))
    alpha = jnp.exp(m - m_new)
    p = jnp.exp(z - m_new)
    l = alpha * l + jnp.sum(p, axis=-1, keepdims=True)
    acc = alpha * acc + jnp.dot(p.astype(BF16), v, preferred_element_type=F32)
    return m_new, l, acc


def _fx_prompt_body(q_ref, k_ref, v_ref, c_ref, ctm_ref, ctr_ref, o_ref, qs, ks, vs):
    _stage_heads(q_ref, k_ref, v_ref, qs, ks, vs)
    t = q_ref.shape[1]
    n_blocks = (t - N_META) // ATTN_BLOCK
    meta = pl.ds(0, N_META)

    def init(tq):
        return (jnp.full((tq, 1), NEG_BIG, F32), jnp.zeros((tq, 1), F32), jnp.zeros((tq, HEAD_DIM), F32))

    for hh in range(2):
        lanes = slice(hh * HEAD_DIM, (hh + 1) * HEAD_DIM)
        ck_meta = ctm_ref[hh:hh + 1, :]
        m, l, acc = _fx_tile(qs[hh, meta, :], ks[hh, meta, :], vs[hh, meta, :],
                             c_ref[hh, meta, :], ck_meta, *init(N_META), True)
        o_ref[0, meta, lanes] = acc / l

        def q_block(qi, carry, hh=hh, lanes=lanes, ck_meta=ck_meta):
            r0 = pl.multiple_of(N_META + qi * ATTN_BLOCK, 16)
            rows = pl.ds(r0, ATTN_BLOCK)
            q = qs[hh, rows, :]
            cq = c_ref[hh, rows, :]
            state = _fx_tile(q, ks[hh, meta, :], vs[hh, meta, :], cq, ck_meta, *init(ATTN_BLOCK), False)

            def k_block(kj, state):
                k0 = pl.multiple_of(N_META + kj * ATTN_BLOCK, 16)
                keys = pl.ds(k0, ATTN_BLOCK)
                c0 = pl.multiple_of(kj * ATTN_BLOCK, ATTN_BLOCK)
                ck = ctr_ref[hh:hh + 1, pl.ds(c0, ATTN_BLOCK)]
                return _fx_tile(q, ks[hh, keys, :], vs[hh, keys, :], cq, ck, *state, False)

            state = lax.fori_loop(0, qi, k_block, state)
            c0 = pl.multiple_of(qi * ATTN_BLOCK, ATTN_BLOCK)
            ck = ctr_ref[hh:hh + 1, pl.ds(c0, ATTN_BLOCK)]
            m, l, acc = _fx_tile(q, ks[hh, rows, :], vs[hh, rows, :], cq, ck, *state, True)
            o_ref[0, rows, lanes] = acc / l
            return carry

        lax.fori_loop(0, n_blocks, q_block, 0)


def _attn_scratch(t):
    return [pltpu.VMEM((2, t, HEAD_DIM), BF16)] * 3


def _sb_prompt(q, k, v):
    b, t, _ = q.shape
    n_pairs = H_SB // 2
    col = pl.BlockSpec((1, t, LANES), lambda i, j: (i, 0, j))
    return pl.pallas_call(
        _sb_prompt_body,
        grid=(b, n_pairs),
        in_specs=[col, col, col],
        out_specs=col,
        out_shape=jax.ShapeDtypeStruct((b, t, W_SB), F32),
        scratch_shapes=_attn_scratch(t),
        compiler_params=_cparams("parallel", "parallel"),
        name="sb_prompt",
    )(q, k, v)


def _fx_prompt(q, k, v, c, ct_meta, ct_reg):
    b, t, _ = q.shape
    n_pairs = H_FOX // 2
    col = pl.BlockSpec((1, t, LANES), lambda i, j: (i, 0, j + H_SB // 2))
    whole = lambda a: pl.BlockSpec((None, None) + a.shape[2:], lambda i, j: (i, j) + (0,) * (a.ndim - 2))
    return pl.pallas_call(
        _fx_prompt_body,
        grid=(b, n_pairs),
        in_specs=[col, col, col, whole(c), whole(ct_meta), whole(ct_reg)],
        out_specs=pl.BlockSpec((1, t, LANES), lambda i, j: (i, 0, j)),
        out_shape=jax.ShapeDtypeStruct((b, t, W_FOX), F32),
        scratch_shapes=_attn_scratch(t),
        compiler_params=_cparams("parallel", "parallel"),
        name="fx_prompt",
    )(q, k, v, c, ct_meta, ct_reg)


def _sample_body(n_new, pt_ref, q_ref, kn_ref, vn_ref, cq_ref, ckn_ref, *refs):
    npg = PAGES_PER_STEP
    k_refs = refs[0:npg]
    v_refs = refs[npg:2 * npg]
    lf_refs = refs[2 * npg:3 * npg]
    o_ref = refs[3 * npg]
    qbd, acc, m_scr, l_scr, tail_scr, rc_scr = refs[3 * npg + 1:]
    rows = N_HEADS * n_new
    sb_rows = H_SB * n_new
    fx_rows = rows - sb_rows
    step = pl.program_id(1)
    tk = npg * PAGE_SIZE

    row_id = lax.broadcasted_iota(jnp.int32, (rows, D_MODEL), 0)
    lane_id = lax.broadcasted_iota(jnp.int32, (rows, D_MODEL), 1)
    own_head = (row_id // n_new) == (lane_id // HEAD_DIM)

    @pl.when(step == 0)
    def _():
        qt = jnp.concatenate([q_ref[...]] * N_HEADS, axis=0)
        qm = jnp.where(own_head, qt, jnp.zeros_like(qt))
        qbd[...] = qm
        s_new = _dot_nt(qm.astype(F32), kn_ref[...])
        qi = lax.broadcasted_iota(jnp.int32, (rows, n_new), 0) % n_new
        ki = lax.broadcasted_iota(jnp.int32, (rows, n_new), 1)
        z = s_new[:sb_rows]
        strict = (ki < qi)[:sb_rows]
        ls = _log_sigmoid(z)
        l1m = jnp.where(strict, ls - z, 0.0)
        suffix = _split_dot(l1m, _suffix_matrix(n_new), 3)
        w_sb = jnp.where(strict, jnp.exp(ls + suffix), 0.0)
        tail_scr[...] = jnp.sum(l1m, axis=-1, keepdims=True)
        zf = s_new[sb_rows:] + cq_ref[0, sb_rows:, :] - ckn_ref[0, sb_rows:, :]
        zf = jnp.where((ki <= qi)[sb_rows:], zf, -jnp.inf)
        m0 = jnp.max(zf, axis=-1, keepdims=True)
        p = jnp.exp(zf - m0)
        m_scr[...] = m0
        l_scr[...] = jnp.sum(p, axis=-1, keepdims=True)
        rc_scr[...] = jnp.zeros_like(rc_scr)
        pw = jnp.concatenate([w_sb, p], axis=0)
        acc[...] = jnp.dot(pw, vn_ref[...], preferred_element_type=F32)

    qm = qbd[...]
    s = jnp.concatenate([_dot_nt(qm, k_refs[j][...].astype(BF16)) for j in range(npg)], axis=1)
    u = _suffix_matrix(tk)
    z = s[:sb_rows]
    ls = _log_sigmoid(z)
    l1m = ls - z
    suffix = _split_dot(l1m, u, 2)
    tail = tail_scr[...]
    w_sb = jnp.exp(ls + suffix + tail)
    tail_scr[...] = tail + suffix[:, 0:1] + l1m[:, 0:1]
    plf = jnp.concatenate([lf_refs[j][...] for j in range(npg)], axis=0)
    eye = (lax.broadcasted_iota(jnp.int32, (H_FOX, H_FOX), 0)
           == lax.broadcasted_iota(jnp.int32, (H_FOX, H_FOX), 1)).astype(BF16)
    plf_t = None
    rem = plf
    for _ in range(3):
        piece = rem.astype(BF16)
        d = _dot_nt(eye, piece)
        plf_t = d if plf_t is None else plf_t + d
        rem = rem - piece.astype(F32)
    rc_prev = rc_scr[...]
    rc = _split_dot(plf_t, u, 3) + rc_prev
    rc_scr[...] = rc_prev + jnp.sum(plf_t, axis=-1, keepdims=True)
    bias = jnp.concatenate([jnp.broadcast_to(rc[h:h + 1, :], (n_new, tk)) for h in range(H_FOX)], axis=0)
    zf = s[sb_rows:] + cq_ref[0, sb_rows:, :] + bias
    m_prev = m_scr[...]
    m_new = jnp.maximum(m_prev, jnp.max(zf, axis=-1, keepdims=True))
    alpha = jnp.exp(m_prev - m_new)
    p = jnp.exp(zf - m_new)
    m_scr[...] = m_new
    l_scr[...] = alpha * l_scr[...] + jnp.sum(p, axis=-1, keepdims=True)
    pw = jnp.concatenate([w_sb, p], axis=0).astype(BF16)
    scale = jnp.concatenate([jnp.ones((sb_rows, 1), F32), alpha], axis=0)
    upd = acc[...] * scale
    for j in range(npg):
        upd = upd + jnp.dot(pw[:, j * PAGE_SIZE:(j + 1) * PAGE_SIZE], v_refs[j][...].astype(BF16),
                            preferred_element_type=F32)
    acc[...] = upd

    @pl.when(step == pl.num_programs(1) - 1)
    def _():
        norm = jnp.concatenate([jnp.ones((sb_rows, 1), F32), 1.0 / l_scr[...]], axis=0)
        full = jnp.where(own_head, acc[...] * norm, 0.0)
        out = full[0:n_new]
        for h in range(1, N_HEADS):
            out = out + full[h * n_new:(h + 1) * n_new]
        o_ref[...] = out


def _sample_attn(layer, page_table, q, k_new, v_new, cq, ckn, cache_k, cache_v, cache_logf, n_new):
    db, n_pages = page_table.shape
    npg = PAGES_PER_STEP
    n_steps = n_pages // npg
    rows = N_HEADS * n_new

    def page_map(j):
        return lambda b, s, pt: (layer, pt[b, n_pages - (s + 1) * npg + j], 0, 0)

    per_row = lambda c: pl.BlockSpec((n_new, c), lambda b, s, pt: (b, 0))
    per_row3 = lambda a: pl.BlockSpec((1,) + a.shape[1:], lambda b, s, pt: (b, 0, 0))
    kv_specs = [pl.BlockSpec((None, None, PAGE_SIZE, D_MODEL), page_map(j)) for j in range(npg)]
    lf_specs = [pl.BlockSpec((None, None, PAGE_SIZE, H_FOX), page_map(j)) for j in range(npg)]
    grid_spec = pltpu.PrefetchScalarGridSpec(
        num_scalar_prefetch=1,
        grid=(db, n_steps),
        in_specs=[per_row(D_MODEL), per_row(D_MODEL), per_row(D_MODEL), per_row3(cq), per_row3(ckn)]
                 + kv_specs + kv_specs + lf_specs,
        out_specs=per_row(D_MODEL),
        scratch_shapes=[pltpu.VMEM((rows, D_MODEL), BF16),
                        pltpu.VMEM((rows, D_MODEL), F32),
                        pltpu.VMEM((rows - H_SB * n_new, 1), F32),
                        pltpu.VMEM((rows - H_SB * n_new, 1), F32),
                        pltpu.VMEM((H_SB * n_new, 1), F32),
                        pltpu.VMEM((H_FOX, 1), F32)],
    )
    return pl.pallas_call(
        functools.partial(_sample_body, n_new),
        grid_spec=grid_spec,
        out_shape=jax.ShapeDtypeStruct((db * n_new, D_MODEL), F32),
        compiler_params=_cparams("parallel", "arbitrary"),
        name="sample_attn",
    )(page_table, q, k_new, v_new, cq, ckn, *([cache_k] * npg), *([cache_v] * npg), *([cache_logf] * npg))


def _merge_body(x_ref, osb_ref, ofx_ref, gate_ref, gsb_ref, gfx_ref, wo_ref, gffn_ref, wr_ref, br_ref,
                xmid_ref, h_ref, logit_ref):
    a_sb = (_rms_scale(osb_ref[...]) * gsb_ref[...]).astype(BF16)
    a_fx = (_rms_scale(ofx_ref[...]) * gfx_ref[...] * jax.nn.sigmoid(gate_ref[...].astype(F32))).astype(BF16)
    y = (jnp.dot(a_sb, wo_ref[:W_SB, :], preferred_element_type=F32)
         + jnp.dot(a_fx, wo_ref[W_SB:, :], preferred_element_type=F32))
    x = x_ref[...] + y
    xmid_ref[...] = x
    h = _rms_scale(x) * gffn_ref[...]
    h_ref[...] = h
    logit_ref[...] = jnp.dot(h, wr_ref[...], preferred_element_type=F32,
                             precision=lax.Precision.HIGHEST) + br_ref[...]


def _merge(x, o_sb, o_fx, gate, g_sb, g_fx, w_out, g_ffn, w_route, b_route, tm):
    n = x.shape[0]
    row = lambda c: pl.BlockSpec((tm, c), lambda i: (i, 0))
    full = lambda a: pl.BlockSpec(a.shape, lambda i: (0,) * a.ndim)
    n_logit = w_route.shape[1]
    return pl.pallas_call(
        _merge_body,
        grid=(n // tm,),
        in_specs=[row(D_MODEL), row(W_SB), row(W_FOX), row(W_FOX), full(g_sb), full(g_fx), full(w_out),
                  full(g_ffn), full(w_route), full(b_route)],
        out_specs=[row(D_MODEL), row(D_MODEL), row(n_logit)],
        out_shape=[jax.ShapeDtypeStruct((n, D_MODEL), F32),
                   jax.ShapeDtypeStruct((n, D_MODEL), F32),
                   jax.ShapeDtypeStruct((n, n_logit), F32)],
        compiler_params=_cparams("parallel"),
        name="merge",
    )(x, o_sb, o_fx, gate, g_sb, g_fx, w_out, g_ffn, w_route, b_route)


def _expert_body(be_ref, x_ref, gate_ref, w1_ref, w3_ref, w2_ref, y_ref):
    x = x_ref[...].astype(BF16)
    a = jnp.dot(x, w1_ref[...], preferred_element_type=F32)
    b = jnp.dot(x, w3_ref[...], preferred_element_type=F32)
    hmid = (a * jax.nn.sigmoid(a) * b).astype(BF16)
    y_ref[...] = jnp.dot(hmid, w2_ref[...], preferred_element_type=F32) * gate_ref[...]


def _experts(block_exp, xb, slot_gate, w1, w3, w2):
    n_slots = xb.shape[0]
    n_blocks = n_slots // MOE_BLOCK
    grid_spec = pltpu.PrefetchScalarGridSpec(
        num_scalar_prefetch=1,
        grid=(n_blocks,),
        in_specs=[pl.BlockSpec((MOE_BLOCK, D_MODEL), lambda i, be: (i, 0)),
                  pl.BlockSpec((MOE_BLOCK, 1), lambda i, be: (i, 0)),
                  pl.BlockSpec((None, D_MODEL, D_EXPERT), lambda i, be: (be[i], 0, 0)),
                  pl.BlockSpec((None, D_MODEL, D_EXPERT), lambda i, be: (be[i], 0, 0)),
                  pl.BlockSpec((None, D_EXPERT, D_MODEL), lambda i, be: (be[i], 0, 0))],
        out_specs=pl.BlockSpec((MOE_BLOCK, D_MODEL), lambda i, be: (i, 0)),
    )
    return pl.pallas_call(
        _expert_body,
        grid_spec=grid_spec,
        out_shape=jax.ShapeDtypeStruct((n_slots, D_MODEL), F32),
        compiler_params=_cparams("arbitrary"),
        name="experts",
    )(block_exp, xb, slot_gate, w1, w3, w2)


def _route(logits):
    g_logits = logits[:, :N_GROUPS]
    grp = jnp.argmax(g_logits, axis=-1).astype(jnp.int32)
    p_grp = jnp.take_along_axis(jax.nn.softmax(g_logits, axis=-1), grp[:, None], axis=1)
    e_logits = logits[:, N_GROUPS:].reshape(-1, N_GROUPS, EXPERTS_PER_GROUP)
    e_logits = jnp.take_along_axis(e_logits, grp[:, None, None], axis=1)[:, 0]
    top_v, top_i = lax.top_k(e_logits, 2)
    gate = p_grp * jax.nn.softmax(top_v, axis=-1)
    eid = grp[:, None] * EXPERTS_PER_GROUP + top_i.astype(jnp.int32)
    return eid, gate


def _moe(h, logits, w1, w3, w2):
    n_tok = h.shape[0]
    eid, gate = _route(logits)
    n_asg = n_tok * 2
    flat_e = eid.reshape(n_asg)
    flat_t = jnp.repeat(jnp.arange(n_tok, dtype=jnp.int32), 2)
    order = jnp.argsort(flat_e)
    se = flat_e[order]
    counts = jnp.bincount(flat_e, length=N_EXPERTS)
    padded = (counts + MOE_BLOCK - 1) // MOE_BLOCK * MOE_BLOCK
    ends_p = jnp.cumsum(padded)
    dest = (ends_p - padded)[se] + jnp.arange(n_asg) - (jnp.cumsum(counts) - counts)[se]
    n_blocks = -(-n_asg // MOE_BLOCK) + N_EXPERTS
    n_slots = n_blocks * MOE_BLOCK
    slot_tok = jnp.full((n_slots,), n_tok, jnp.int32).at[dest].set(flat_t[order])
    slot_gate = jnp.zeros((n_slots,), F32).at[dest].set(gate.reshape(n_asg)[order])
    block_exp = jnp.minimum(jnp.searchsorted(ends_p, jnp.arange(n_blocks) * MOE_BLOCK, side='right'),
                            N_EXPERTS - 1).astype(jnp.int32)
    hp = jnp.concatenate([h, jnp.zeros((1, D_MODEL), h.dtype)], axis=0)
    xb = hp[slot_tok]
    yb = _experts(block_exp, xb, slot_gate[:, None], w1, w3, w2)
    y = jnp.zeros((n_tok + 1, D_MODEL), F32).at[slot_tok].add(yb)
    return y[:n_tok]


def _final_body(x_ref, y_ref, g_ref, o_ref):
    o_ref[...] = _rms_scale(x_ref[...] + y_ref[...]) * g_ref[...]


def _final_norm(x, y, g, tm):
    n = x.shape[0]
    row = pl.BlockSpec((tm, D_MODEL), lambda i: (i, 0))
    return pl.pallas_call(
        _final_body,
        grid=(n // tm,),
        in_specs=[row, row, pl.BlockSpec(g.shape, lambda i: (0, 0))],
        out_specs=row,
        out_shape=jax.ShapeDtypeStruct((n, D_MODEL), F32),
        compiler_params=_cparams("parallel"),
        name="final_norm",
    )(x, y, g)


def _row_tile(n):
    for tm in (768, 512, 384, 256, 128, 64, 32, 16, 8):
        if n % tm == 0:
            return tm
    raise ValueError(f"row count {n} is not a multiple of 8")


def kernel(x_prompt, x_sample, cache_k, cache_v, cache_logf, page_table, meta_tokens, norm_mix_g, w_in, b_forget,
           norm_sb_g, norm_fox_g, w_out, norm_ffn_g, w_group, b_group, w_router, b_router, w_exp_gate, w_exp_up,
           w_exp_down, norm_final_g):
    bsz, seq, _ = x_prompt.shape
    db, n_new, _ = x_sample.shape
    depth = w_in.shape[0]
    t = seq + N_META
    n_p = bsz * t
    n_s = db * n_new
    n_pool = cache_k.shape[1]
    assert (seq % ATTN_BLOCK) == 0 and page_table.shape[1] % PAGES_PER_STEP == 0

    meta = jnp.broadcast_to(meta_tokens[None].astype(x_prompt.dtype), (bsz, N_META, D_MODEL))
    xp = jnp.concatenate([meta, x_prompt], axis=1).reshape(n_p, D_MODEL)
    xs = x_sample.reshape(n_s, D_MODEL)
    ck_pages = cache_k.reshape(depth, n_pool, PAGE_SIZE, D_MODEL)
    cv_pages = cache_v.reshape(depth, n_pool, PAGE_SIZE, D_MODEL)
    tm_p, tm_s = _row_tile(n_p), _row_tile(n_s)
    n_main = 3 * D_MODEL + W_FOX

    streams = {"p": (xp, ()), "s": (xs, ())}
    outs = {name: {"k": [], "v": [], "lf": []} for name in streams}
    for l in range(depth):
        w_main = w_in[l, :, :n_main].astype(BF16)
        w_f = w_in[l, :, n_main:].astype(BF16)
        b_f = b_forget[l][None, :]
        g_mix = norm_mix_g[l][None, :]
        w_o = w_out[l].astype(BF16)
        w_route = jnp.concatenate([w_group[l], w_router[l]], axis=1)
        b_route = jnp.concatenate([b_group[l], b_router[l]])[None, :]
        w1 = w_exp_gate[l].astype(BF16)
        w3 = w_exp_up[l].astype(BF16)
        w2 = w_exp_down[l].astype(BF16)
        new_streams = {}
        for name, tm in (("p", tm_p), ("s", tm_s)):
            x, res = streams[name]
            r = _inproj(x, res, g_mix, w_main, w_f, b_f, tm)
            if res:
                x, r = r[0], r[1:]
            q, k, v, gate, logf = r
            outs[name]["k"].append(k)
            outs[name]["v"].append(v)
            outs[name]["lf"].append(logf)
            if name == "p":
                q3, k3, v3 = (a.reshape(bsz, t, D_MODEL) for a in (q, k, v))
                c = jnp.cumsum(logf.reshape(bsz, t, H_FOX), axis=1)
                ct = jnp.transpose(c, (0, 2, 1)).reshape(bsz, H_FOX // 2, 2, t)
                o_sb = _sb_prompt(q3, k3, v3).reshape(n_p, W_SB)
                o_fx = _fx_prompt(q3, k3, v3, ct[..., None], ct[..., :N_META], ct[..., N_META:]).reshape(n_p, W_FOX)
            else:
                cn = jnp.cumsum(logf.reshape(db, n_new, H_FOX), axis=1)
                cn_hi = jnp.transpose(cn, (0, 2, 1))
                zeros = jnp.zeros((db, H_SB * n_new), F32)
                cq = jnp.concatenate([zeros, cn_hi.reshape(db, H_FOX * n_new)], axis=1)[:, :, None]
                ckn = jnp.concatenate(
                    [jnp.zeros((db, H_SB * n_new, n_new), F32),
                     jnp.repeat(cn_hi, n_new, axis=1)], axis=1)
                o = _sample_attn(l, page_table, q, k, v, cq, ckn, ck_pages, cv_pages, cache_logf, n_new)
                o_sb, o_fx = o[:, :W_SB], o[:, W_SB:]
            x_mid, h, logits = _merge(x, o_sb, o_fx, gate, norm_sb_g[l][None, :], norm_fox_g[l][None, :], w_o,
                                      norm_ffn_g[l][None, :], w_route, b_route, tm)
            y = _moe(h, logits, w1, w3, w2)
            new_streams[name] = (x_mid, (y,))
        streams = new_streams

    g_fin = norm_final_g[None, :]
    y_prompt = _final_norm(streams["p"][0], streams["p"][1][0], g_fin, tm_p).reshape(bsz, t, D_MODEL)[:, N_META:]
    y_sample = _final_norm(streams["s"][0], streams["s"][1][0], g_fin, tm_s).reshape(db, n_new, D_MODEL)
    heads_p = (bsz, t, N_HEADS, HEAD_DIM)
    heads_s = (db, n_new, N_HEADS, HEAD_DIM)
    return (y_prompt, y_sample,
            jnp.stack([a.reshape(heads_p) for a in outs["p"]["k"]]),
            jnp.stack([a.reshape(heads_p) for a in outs["p"]["v"]]),
            jnp.stack([a.reshape(bsz, t, H_FOX) for a in outs["p"]["lf"]]),
            jnp.stack([a.reshape(heads_s) for a in outs["s"]["k"]]),
            jnp.stack([a.reshape(heads_s) for a in outs["s"]["v"]]),
            jnp.stack([a.reshape(db, n_new, H_FOX) for a in outs["s"]["lf"]]))
```

```python
import functools

import jax
import jax.numpy as jnp
from jax import lax
from jax.experimental import pallas as pl
from jax.experimental.pallas import tpu as pltpu

F32 = jnp.float32
BF16 = jnp.bfloat16

D_MODEL = 1024
HEAD_DIM = 64
H_SB = 8
H_FOX = 8
N_HEADS = H_SB + H_FOX
W_SB = H_SB * HEAD_DIM
W_FOX = H_FOX * HEAD_DIM
N_META = 16
N_GROUPS = 4
EXPERTS_PER_GROUP = 8
N_EXPERTS = N_GROUPS * EXPERTS_PER_GROUP
D_EXPERT = D_MODEL // 2
MOE_BLOCK = 256
RMS_EPS = 1e-6
PAGE_SIZE = 128
LANES = 128
SUBLANES = 8
ROW_TILES = D_MODEL // LANES
ATTN_BLOCK = 256
Q_TILE = 128
PAGES_PER_STEP = 4
NEG_BIG = -1e30
VMEM_LIMIT = 56 * 1024 * 1024


def _cparams(*sem):
    return pltpu.CompilerParams(dimension_semantics=sem, vmem_limit_bytes=VMEM_LIMIT)


def _log_sigmoid_fast(z):
    return jnp.minimum(z, 0.0) - jnp.log(1.0 + jnp.exp(-jnp.abs(z)))


def _log_sigmoid(x):
    return jnp.minimum(x, 0.0) - jnp.log1p(jnp.exp(-jnp.abs(x)))


def _rms_scale(x):
    return x * lax.rsqrt(jnp.mean(x * x, axis=-1, keepdims=True) + RMS_EPS)


def _dot_nt(a, b):
    return lax.dot_general(a, b, (((1,), (1,)), ((), ())), preferred_element_type=F32)


def _tri(n, kind):
    a = lax.broadcasted_iota(jnp.int32, (n, n), 0)
    b = lax.broadcasted_iota(jnp.int32, (n, n), 1)
    m = {"suffix": a > b, "prefix": b <= a, "eye": a == b}[kind]
    return jnp.where(m, 1.0, 0.0).astype(BF16)


def _pieces(x, parts):
    out = []
    rem = x
    for p in range(parts):
        piece = rem.astype(BF16)
        out.append(piece)
        if p + 1 < parts:
            rem = rem - piece.astype(F32)
    return out


def _split_dot(x, m01, parts):
    return sum(jnp.dot(p, m01, preferred_element_type=F32) for p in _pieces(x, parts))


def _split_dot_rhs(m01, x, parts):
    return sum(jnp.dot(m01, p, preferred_element_type=F32) for p in _pieces(x, parts))


def _transpose_small(x, parts=3):
    eye = _tri(x.shape[1], "eye")
    return sum(_dot_nt(eye, p) for p in _pieces(x, parts))


def _load_row_tiled(ref, rows):
    return jnp.concatenate([ref[pl.ds(s, rows, stride=ROW_TILES), :] for s in range(ROW_TILES)], axis=1)


def _store_row_tiled(ref, x, lead=()):
    rows = x.shape[0]
    for s in range(ROW_TILES):
        ref[lead + (pl.ds(s, rows, stride=ROW_TILES), slice(None))] = x[:, s * LANES:(s + 1) * LANES]


def _inproj_body(has_res, *refs):
    if has_res:
        x_ref, ya_ref, yb_ref, g_ref, w_ref, wf_ref, bf_ref, xo_ref, q_ref, k_ref, v_ref, gate_ref, logf_ref = refs
        rows = x_ref.shape[0]
        x = x_ref[...] + (_load_row_tiled(ya_ref, rows) + _load_row_tiled(yb_ref, rows))
        xo_ref[...] = x
    else:
        x_ref, g_ref, w_ref, wf_ref, bf_ref, q_ref, k_ref, v_ref, gate_ref, logf_ref = refs
        x = x_ref[...]
    h = (_rms_scale(x) * g_ref[...]).astype(BF16)
    half = D_MODEL // 2
    q_scale = HEAD_DIM ** -0.5
    for c in range(7):
        z = jnp.dot(h, w_ref[:, c * half:(c + 1) * half], preferred_element_type=F32)
        dst = pl.ds((c % 2) * half, half)
        if c < 2:
            q_ref[:, dst] = (z * q_scale).astype(BF16)
        elif c < 4:
            k_ref[:, dst] = z
        elif c < 6:
            v_ref[:, dst] = z
        else:
            gate_ref[...] = z.astype(BF16)
    zf = jnp.dot(h, wf_ref[...], preferred_element_type=F32) + bf_ref[...]
    logf_ref[...] = _log_sigmoid(zf)


def _inproj(x, y2, g, w_main, w_f, b_f, tm):
    n = x.shape[0]
    nt = n // tm
    row = lambda c: pl.BlockSpec((tm, c), lambda i: (i, 0))
    full = lambda a: pl.BlockSpec(a.shape, lambda i: (0,) * a.ndim)
    out_shape = [jax.ShapeDtypeStruct((n, D_MODEL), BF16),
                 jax.ShapeDtypeStruct((n, D_MODEL), F32),
                 jax.ShapeDtypeStruct((n, D_MODEL), F32),
                 jax.ShapeDtypeStruct((n, W_FOX), BF16),
                 jax.ShapeDtypeStruct((n, H_FOX), F32)]
    out_specs = [row(D_MODEL), row(D_MODEL), row(D_MODEL), row(W_FOX), row(H_FOX)]
    in_specs = [row(D_MODEL)]
    args = [x]
    if y2 is not None:
        out_shape = [jax.ShapeDtypeStruct((n, D_MODEL), F32)] + out_shape
        out_specs = [row(D_MODEL)] + out_specs
        in_specs += [pl.BlockSpec((tm * ROW_TILES, LANES), lambda i: (i, 0)),
                     pl.BlockSpec((tm * ROW_TILES, LANES), lambda i: (i + nt, 0))]
        args += [y2, y2]
    return pl.pallas_call(
        functools.partial(_inproj_body, y2 is not None),
        grid=(nt,),
        in_specs=in_specs + [full(g), full(w_main), full(w_f), full(b_f)],
        out_specs=out_specs,
        out_shape=out_shape,
        compiler_params=_cparams("parallel"),
        name="inproj",
    )(*args, g, w_main, w_f, b_f)


def _forget_cumsum_body(lf_ref, ccol_ref, crow_meta_ref, crow_ref):
    t = lf_ref.shape[1]
    n_blocks = (t - N_META) // ATTN_BLOCK
    meta = pl.ds(0, N_META)
    c = _split_dot_rhs(_tri(N_META, "prefix"), lf_ref[0, meta, :], 3)
    ccol_ref[0, meta, :] = c
    crow_meta_ref[0] = _transpose_small(c)
    carry0 = c[N_META - 1:N_META, :]
    prefix = _tri(ATTN_BLOCK, "prefix")

    def block(i, carry):
        r0 = pl.multiple_of(N_META + i * ATTN_BLOCK, 16)
        rows = pl.ds(r0, ATTN_BLOCK)
        c = _split_dot_rhs(prefix, lf_ref[0, rows, :], 3) + carry
        ccol_ref[0, rows, :] = c
        crow_ref[0, :, pl.ds(pl.multiple_of(i * ATTN_BLOCK, ATTN_BLOCK), ATTN_BLOCK)] = _transpose_small(c)
        return c[ATTN_BLOCK - 1:ATTN_BLOCK, :]

    lax.fori_loop(0, n_blocks, block, carry0)


def _forget_cumsum(logf):
    b, t, _ = logf.shape
    return pl.pallas_call(
        _forget_cumsum_body,
        grid=(b,),
        in_specs=[pl.BlockSpec((1, t, H_FOX), lambda i: (i, 0, 0))],
        out_specs=[pl.BlockSpec((1, t, H_FOX), lambda i: (i, 0, 0)),
                   pl.BlockSpec((1, H_FOX, N_META), lambda i: (i, 0, 0)),
                   pl.BlockSpec((1, H_FOX, t - N_META), lambda i: (i, 0, 0))],
        out_shape=[jax.ShapeDtypeStruct((b, t, H_FOX), F32),
                   jax.ShapeDtypeStruct((b, H_FOX, N_META), F32),
                   jax.ShapeDtypeStruct((b, H_FOX, t - N_META), F32)],
        compiler_params=_cparams("parallel"),
        name="forget_cumsum",
    )(logf)


def _sb_tile(q, k, v, tail, diag, off=0):
    tq, tk = q.shape[0], k.shape[0]
    z = _dot_nt(q, k)
    ls = _log_sigmoid_fast(z)
    l1m = ls - z
    if diag:
        row = lax.broadcasted_iota(jnp.int32, (tq, tk), 0)
        col = lax.broadcasted_iota(jnp.int32, (tq, tk), 1)
        valid = col < row + off
        l1m = jnp.where(valid, l1m, 0.0)
    suffix = _split_dot(l1m, _tri(tk, "suffix"), 2)
    w = jnp.exp(ls + suffix + tail)
    if diag:
        w = jnp.where(valid, w, 0.0)
    o = jnp.dot(w.astype(BF16), v, preferred_element_type=F32)
    tail = tail + suffix[:, 0:1] + l1m[:, 0:1]
    return o, tail


def _fx_tile(q, k, v, cq, ck, m, l, acc, diag, off=0):
    tq, tk = q.shape[0], k.shape[0]
    z = _dot_nt(q, k) + (cq - ck)
    if diag:
        row = lax.broadcasted_iota(jnp.int32, (tq, tk), 0)
        col = lax.broadcasted_iota(jnp.int32, (tq, tk), 1)
        z = jnp.where(col <= row + off, z, -jnp.inf)
    m_new = jnp.maximum(m, jnp.max(z, axis=-1, keepdims=True))
    alpha = jnp.exp(m - m_new)
    p = jnp.exp(z - m_new)
    l = alpha * l + jnp.sum(p, axis=-1, keepdims=True)
    acc = alpha * acc + jnp.dot(p.astype(BF16), v, preferred_element_type=F32)
    return m_new, l, acc


def _prompt_attn_body(qsb_ref, ksb_ref, vsb_ref, qfx_ref, kfx_ref, vfx_ref, ccol_ref, crm_ref, crr_ref,
                      osb_ref, ofx_ref, qs, ks, vs, acc, col):
    for g, (qr, kr, vr) in enumerate(((qsb_ref, ksb_ref, vsb_ref), (qfx_ref, kfx_ref, vfx_ref))):
        for hh in range(2):
            lanes = slice(hh * HEAD_DIM, (hh + 1) * HEAD_DIM)
            qs[2 * g + hh] = qr[0, :, lanes]
            ks[2 * g + hh] = kr[0, :, lanes].astype(BF16)
            vs[2 * g + hh] = vr[0, :, lanes].astype(BF16)
    t = qsb_ref.shape[1]
    n_blocks = (t - N_META) // Q_TILE
    pair = pl.program_id(1)
    meta = pl.ds(0, N_META)
    head_lane = lax.broadcasted_iota(jnp.int32, (1, H_FOX), 1)

    def query_bias(rows, hh):
        sel = head_lane == pair * 2 + hh
        return jnp.sum(jnp.where(sel, ccol_ref[0, rows, :], 0.0), axis=1, keepdims=True)

    def key_bias_meta(hh):
        return crm_ref[0, pl.ds(pair * 2 + hh, 1), :]

    def fx_init(tq):
        return (jnp.full((tq, 1), NEG_BIG, F32), jnp.zeros((tq, 1), F32), jnp.zeros((tq, HEAD_DIM), F32))

    for hh in range(2):
        lanes = slice(hh * HEAD_DIM, (hh + 1) * HEAD_DIM)
        o_meta, _ = _sb_tile(qs[hh, meta, :], ks[hh, meta, :], vs[hh, meta, :], jnp.zeros((N_META, 1), F32), True)
        osb_ref[0, meta, lanes] = o_meta
        g = 2 + hh
        _, l, a = _fx_tile(qs[g, meta, :], ks[g, meta, :], vs[g, meta, :], query_bias(meta, hh), key_bias_meta(hh),
                           *fx_init(N_META), True)
        ofx_ref[0, meta, lanes] = a / l

    def tiles(rows, keys, ck_of, diag, first, off=0):
        tq = Q_TILE
        tk = keys.size
        k = [ks[g, keys, :] for g in range(4)]
        z = [_dot_nt(qs[g, rows, :], k[g]) for g in range(4)]
        if diag:
            row = lax.broadcasted_iota(jnp.int32, (tq, tk), 0)
            col_id = lax.broadcasted_iota(jnp.int32, (tq, tk), 1)
            strict = col_id < row + off
            incl = col_id <= row + off
        ls = [_log_sigmoid_fast(z[hh]) for hh in range(2)]
        l1m = [ls[hh] - z[hh] for hh in range(2)]
        if diag:
            l1m = [jnp.where(strict, x, 0.0) for x in l1m]
        parts = [p for hh in range(2) for p in _pieces(l1m[hh], 2)]
        sums = jnp.dot(jnp.concatenate(parts, axis=0), _tri(tk, "suffix"), preferred_element_type=F32)
        suffix = [sums[(2 * hh) * tq:(2 * hh + 1) * tq] + sums[(2 * hh + 1) * tq:(2 * hh + 2) * tq]
                  for hh in range(2)]
        fx = []
        for hh in range(2):
            g = 2 + hh
            m_old, l_old, a_old = fx_init(tq) if first else (col[2 + hh], col[4 + hh], acc[g])
            zf = z[g] + (col[6 + hh] - ck_of(hh))
            if diag:
                zf = jnp.where(incl, zf, -jnp.inf)
            m_new = jnp.maximum(m_old, jnp.max(zf, axis=-1, keepdims=True))
            alpha = jnp.exp(m_old - m_new)
            p = jnp.exp(zf - m_new)
            col[2 + hh] = m_new
            col[4 + hh] = alpha * l_old + jnp.sum(p, axis=-1, keepdims=True)
            fx.append((alpha * a_old, p.astype(BF16)))
        for hh in range(2):
            g = 2 + hh
            acc[g] = fx[hh][0] + jnp.dot(fx[hh][1], vs[g, keys, :], preferred_element_type=F32)
        for hh in range(2):
            tail = jnp.zeros((tq, 1), F32) if first else col[hh]
            w = jnp.exp(ls[hh] + suffix[hh] + tail)
            if diag:
                w = jnp.where(strict, w, 0.0)
            o = jnp.dot(w.astype(BF16), vs[hh, keys, :], preferred_element_type=F32)
            acc[hh] = o if first else acc[hh] + o
            col[hh] = tail + suffix[hh][:, 0:1] + l1m[hh][:, 0:1]

    def q_block(qi, carry):
        r0 = pl.multiple_of(N_META + qi * Q_TILE, 16)
        rows = pl.ds(r0, Q_TILE)
        for hh in range(2):
            col[6 + hh] = query_bias(rows, hh)

        def key_tile(kj):
            return pl.ds(pl.multiple_of(N_META + kj * ATTN_BLOCK, 16), ATTN_BLOCK)

        def key_bias(kj):
            c0 = pl.multiple_of(kj * ATTN_BLOCK, ATTN_BLOCK)
            return lambda hh: crr_ref[0, pl.ds(pair * 2 + hh, 1), pl.ds(c0, ATTN_BLOCK)]

        kd = (qi * Q_TILE) // ATTN_BLOCK
        tiles(rows, key_tile(kd), key_bias(kd), True, True, qi * Q_TILE - kd * ATTN_BLOCK)

        def k_block(step, c):
            kj = kd - 1 - step
            tiles(rows, key_tile(kj), key_bias(kj), False, False)
            return c

        lax.fori_loop(0, kd, k_block, 0)
        tiles(rows, meta, key_bias_meta, False, False)
        for hh in range(2):
            lanes = slice(hh * HEAD_DIM, (hh + 1) * HEAD_DIM)
            osb_ref[0, rows, lanes] = acc[hh]
            ofx_ref[0, rows, lanes] = acc[2 + hh] / col[4 + hh]
        return carry

    lax.fori_loop(0, n_blocks, q_block, 0)


def _prompt_attn(q, k, v, ccol, crow_meta, crow):
    b, t, _ = q.shape
    n_pairs = H_SB // 2
    sb_col = pl.BlockSpec((1, t, LANES), lambda i, j: (i, 0, j))
    fx_col = pl.BlockSpec((1, t, LANES), lambda i, j: (i, 0, j + n_pairs))
    whole = lambda a: pl.BlockSpec((1,) + a.shape[1:], lambda i, j: (i, 0, 0))
    return pl.pallas_call(
        _prompt_attn_body,
        grid=(b, n_pairs),
        in_specs=[sb_col, sb_col, sb_col, fx_col, fx_col, fx_col, whole(ccol), whole(crow_meta), whole(crow)],
        out_specs=[sb_col, sb_col],
        out_shape=[jax.ShapeDtypeStruct((b, t, W_SB), F32), jax.ShapeDtypeStruct((b, t, W_FOX), F32)],
        scratch_shapes=[pltpu.VMEM((4, t, HEAD_DIM), BF16)] * 3
                       + [pltpu.VMEM((4, Q_TILE, HEAD_DIM), F32), pltpu.VMEM((8, Q_TILE, 1), F32)],
        compiler_params=_cparams("parallel", "parallel"),
        name="prompt_attn",
    )(q, k, v, q, k, v, ccol, crow_meta, crow)


def _sample_body(n_new, pt_ref, q_ref, kn_ref, vn_ref, lfn_ref, *refs):
    npg = PAGES_PER_STEP
    k_refs = refs[0:npg]
    v_refs = refs[npg:2 * npg]
    lf_refs = refs[2 * npg:3 * npg]
    o_ref = refs[3 * npg]
    q_scr, acc, m_scr, l_scr, tail_scr, rc_scr, cq_scr = refs[3 * npg + 1:]
    sb_rows = H_SB * n_new
    step = pl.program_id(1)
    tk = npg * PAGE_SIZE
    head_rows = lambda x, h: x[h * n_new:(h + 1) * n_new]
    head_lanes = lambda h: slice(h * HEAD_DIM, (h + 1) * HEAD_DIM)

    @pl.when(step == 0)
    def _():
        for h in range(N_HEADS):
            q_scr[h] = q_ref[:, head_lanes(h)].astype(F32)
        s_new = jnp.concatenate([_dot_nt(q_scr[h], kn_ref[:, head_lanes(h)]) for h in range(N_HEADS)], axis=0)
        qi = lax.broadcasted_iota(jnp.int32, (N_HEADS * n_new, n_new), 0) % n_new
        ki = lax.broadcasted_iota(jnp.int32, (N_HEADS * n_new, n_new), 1)
        z = s_new[:sb_rows]
        strict = (ki < qi)[:sb_rows]
        ls = _log_sigmoid_fast(z)
        l1m = jnp.where(strict, ls - z, 0.0)
        suffix = _split_dot(l1m, _tri(n_new, "suffix"), 3)
        w_sb = jnp.where(strict, jnp.exp(ls + suffix), 0.0)
        tail_scr[...] = jnp.sum(l1m, axis=-1, keepdims=True)
        cn = _split_dot_rhs(_tri(n_new, "prefix"), lfn_ref[...], 3)
        cn_t = _transpose_small(cn)
        cq = jnp.concatenate([cn[:, h:h + 1] for h in range(H_FOX)], axis=0)
        ck = jnp.concatenate([jnp.broadcast_to(cn_t[h:h + 1, :], (n_new, n_new)) for h in range(H_FOX)], axis=0)
        cq_scr[...] = cq
        zf = jnp.where((ki <= qi)[sb_rows:], s_new[sb_rows:] + (cq - ck), -jnp.inf)
        m0 = jnp.max(zf, axis=-1, keepdims=True)
        p = jnp.exp(zf - m0)
        m_scr[...] = m0
        l_scr[...] = jnp.sum(p, axis=-1, keepdims=True)
        rc_scr[...] = jnp.zeros_like(rc_scr)
        pw = jnp.concatenate([w_sb, p], axis=0)
        acc[...] = jnp.concatenate(
            [jnp.dot(head_rows(pw, h), vn_ref[:, head_lanes(h)], preferred_element_type=F32)
             for h in range(N_HEADS)], axis=0)

    def head_keys(refs_, h):
        return jnp.concatenate([r[pl.ds(h, PAGE_SIZE, stride=N_HEADS), :].astype(BF16) for r in refs_], axis=0)

    s = jnp.concatenate([_dot_nt(q_scr[h].astype(BF16), head_keys(k_refs, h)) for h in range(N_HEADS)], axis=0)
    u = _tri(tk, "suffix")
    z = s[:sb_rows]
    ls = _log_sigmoid_fast(z)
    l1m = ls - z
    suffix = _split_dot(l1m, u, 2)
    tail = tail_scr[...]
    w_sb = jnp.exp(ls + suffix + tail)
    tail_scr[...] = tail + suffix[:, 0:1] + l1m[:, 0:1]
    plf = jnp.concatenate([lf_refs[j][...] for j in range(npg)], axis=0)
    plf_t = _transpose_small(plf)
    rc_prev = rc_scr[...]
    rc = _split_dot(plf_t, u, 3) + rc_prev
    rc_scr[...] = rc_prev + jnp.sum(plf_t, axis=-1, keepdims=True)
    bias = jnp.concatenate([jnp.broadcast_to(rc[h:h + 1, :], (n_new, tk)) for h in range(H_FOX)], axis=0)
    zf = s[sb_rows:] + (cq_scr[...] + bias)
    m_prev = m_scr[...]
    m_new = jnp.maximum(m_prev, jnp.max(zf, axis=-1, keepdims=True))
    alpha = jnp.exp(m_prev - m_new)
    p = jnp.exp(zf - m_new)
    m_scr[...] = m_new
    l_scr[...] = alpha * l_scr[...] + jnp.sum(p, axis=-1, keepdims=True)
    pw = jnp.concatenate([w_sb, p], axis=0)
    scale = jnp.concatenate([jnp.ones((sb_rows, 1), F32), alpha], axis=0)
    pv = jnp.concatenate(
        [jnp.dot(head_rows(pw, h).astype(BF16), head_keys(v_refs, h), preferred_element_type=F32)
         for h in range(N_HEADS)], axis=0)
    acc[...] = acc[...] * scale + pv

    @pl.when(step == pl.num_programs(1) - 1)
    def _():
        norm = jnp.concatenate([jnp.ones((sb_rows, 1), F32), 1.0 / l_scr[...]], axis=0)
        out = acc[...] * norm
        for h in range(N_HEADS):
            o_ref[:, head_lanes(h)] = head_rows(out, h)


def _sample_attn(layer, page_table, q, k_new, v_new, logf_new, cache_k, cache_v, cache_logf, n_new):
    db, n_pages = page_table.shape
    n_pool = cache_logf.shape[1]
    npg = PAGES_PER_STEP
    n_steps = n_pages // npg
    rows = N_HEADS * n_new
    fx_rows = H_FOX * n_new

    def page_of(b, s, pt, j):
        return pt[b, n_pages - (s + 1) * npg + j]

    per_row = lambda c: pl.BlockSpec((n_new, c), lambda b, s, pt: (b, 0))
    kv_specs = [pl.BlockSpec((PAGE_SIZE * N_HEADS, HEAD_DIM),
                             functools.partial(lambda b, s, pt, j: (layer * n_pool + page_of(b, s, pt, j), 0), j=j))
                for j in range(npg)]
    lf_specs = [pl.BlockSpec((None, None, PAGE_SIZE, H_FOX),
                             functools.partial(lambda b, s, pt, j: (layer, page_of(b, s, pt, j), 0, 0), j=j))
                for j in range(npg)]
    grid_spec = pltpu.PrefetchScalarGridSpec(
        num_scalar_prefetch=1,
        grid=(db, n_steps),
        in_specs=[per_row(D_MODEL), per_row(D_MODEL), per_row(D_MODEL), per_row(H_FOX)]
                 + kv_specs + kv_specs + lf_specs,
        out_specs=per_row(D_MODEL),
        scratch_shapes=[pltpu.VMEM((N_HEADS, n_new, HEAD_DIM), F32),
                        pltpu.VMEM((rows, HEAD_DIM), F32),
                        pltpu.VMEM((fx_rows, 1), F32),
                        pltpu.VMEM((fx_rows, 1), F32),
                        pltpu.VMEM((rows - fx_rows, 1), F32),
                        pltpu.VMEM((H_FOX, 1), F32),
                        pltpu.VMEM((fx_rows, 1), F32)],
    )
    return pl.pallas_call(
        functools.partial(_sample_body, n_new),
        grid_spec=grid_spec,
        out_shape=jax.ShapeDtypeStruct((db * n_new, D_MODEL), F32),
        compiler_params=_cparams("parallel", "arbitrary"),
        name="sample_attn",
    )(page_table, q, k_new, v_new, logf_new, *([cache_k] * npg), *([cache_v] * npg), *([cache_logf] * npg))


def _merge_body(x_ref, o_ref, gate_ref, gsb_ref, gfx_ref, wo_ref, gffn_ref, wr_ref, br_ref,
                xmid_ref, h_ref, logit_ref):
    a_sb = (_rms_scale(o_ref[:, :W_SB]) * gsb_ref[...]).astype(BF16)
    a_fx = (_rms_scale(o_ref[:, W_SB:]) * gfx_ref[...] * jax.nn.sigmoid(gate_ref[...].astype(F32))).astype(BF16)
    y = (jnp.dot(a_sb, wo_ref[:W_SB, :], preferred_element_type=F32)
         + jnp.dot(a_fx, wo_ref[W_SB:, :], preferred_element_type=F32))
    x = x_ref[...] + y
    xmid_ref[...] = x
    h = _rms_scale(x) * gffn_ref[...]
    _store_row_tiled(h_ref, h)
    logit_ref[...] = jnp.dot(h, wr_ref[...], preferred_element_type=F32,
                             precision=lax.Precision.HIGHEST) + br_ref[...]


def _merge(x, o_parts, gate, g_sb, g_fx, w_out, g_ffn, w_route, b_route, tm):
    n = x.shape[0]
    row = lambda c: pl.BlockSpec((tm, c), lambda i: (i, 0))
    full = lambda a: pl.BlockSpec(a.shape, lambda i: (0,) * a.ndim)
    n_logit = w_route.shape[1]
    body = _merge_body
    if len(o_parts) == 2:
        def body(x_ref, osb_ref, ofx_ref, *rest):
            return _merge_body(x_ref, _Halves(osb_ref, ofx_ref), *rest)
    return pl.pallas_call(
        body,
        grid=(n // tm,),
        in_specs=[row(D_MODEL)] + [row(o.shape[1]) for o in o_parts]
                 + [row(W_FOX), full(g_sb), full(g_fx), full(w_out), full(g_ffn), full(w_route), full(b_route)],
        out_specs=[row(D_MODEL), pl.BlockSpec((tm * ROW_TILES, LANES), lambda i: (i, 0)), row(n_logit)],
        out_shape=[jax.ShapeDtypeStruct((n, D_MODEL), F32),
                   jax.ShapeDtypeStruct((n * ROW_TILES, LANES), F32),
                   jax.ShapeDtypeStruct((n, n_logit), F32)],
        compiler_params=_cparams("parallel"),
        name="merge",
    )(x, *o_parts, gate, g_sb, g_fx, w_out, g_ffn, w_route, b_route)


class _Halves:
    def __init__(self, lo, hi):
        self.lo, self.hi = lo, hi

    def __getitem__(self, idx):
        rows, cols = idx
        if cols == slice(None, W_SB):
            return self.lo[rows, :]
        assert cols == slice(W_SB, None)
        return self.hi[rows, :]


def _expert_body(n_tok, be_ref, nv_ref, dst_ref, h_hbm, gate_ref, w1_ref, w3_ref, w2_ref, y_hbm,
                 xbuf, ybuf, gsem, ssem):
    i = pl.program_id(0)
    nb = pl.num_programs(0)
    slot = i % 2
    tile = lambda r: pl.ds(pl.multiple_of(r * ROW_TILES, ROW_TILES), ROW_TILES)

    def gather(b, sl):
        def one(r, c):
            d = dst_ref[b * MOE_BLOCK + r]
            tok = jnp.where(d >= n_tok, d - n_tok, d)
            pltpu.make_async_copy(h_hbm.at[tile(tok), :], xbuf.at[sl, tile(r), :], gsem.at[sl]).start()
            return c
        lax.fori_loop(0, nv_ref[b], one, 0)

    def gather_wait(b, sl):
        n = nv_ref[b] * ROW_TILES

        @pl.when(n > 0)
        def _():
            pltpu.make_async_copy(h_hbm.at[pl.ds(0, n), :], xbuf.at[sl, pl.ds(0, n), :], gsem.at[sl]).wait()

    def scatter(b, sl):
        def one(r, c):
            d = dst_ref[b * MOE_BLOCK + r]
            pltpu.make_async_copy(ybuf.at[sl, tile(r), :], y_hbm.at[tile(d), :], ssem.at[sl]).start()
            return c
        lax.fori_loop(0, nv_ref[b], one, 0)

    def scatter_wait(b, sl):
        n = nv_ref[b] * ROW_TILES

        @pl.when(n > 0)
        def _():
            pltpu.make_async_copy(ybuf.at[sl, pl.ds(0, n), :], y_hbm.at[pl.ds(0, n), :], ssem.at[sl]).wait()

    @pl.when(i == 0)
    def _():
        xbuf[...] = jnp.zeros_like(xbuf)
        gather(0, 0)

    @pl.when(i + 1 < nb)
    def _():
        gather(i + 1, 1 - slot)

    gather_wait(i, slot)

    @pl.when(i >= 2)
    def _():
        scatter_wait(i - 2, slot)

    @pl.when(nv_ref[i] > 0)
    def _():
        x = _load_row_tiled(xbuf.at[slot], MOE_BLOCK).astype(BF16)
        a = jnp.dot(x, w1_ref[...], preferred_element_type=F32)
        b = jnp.dot(x, w3_ref[...], preferred_element_type=F32)
        hmid = (a * jax.nn.sigmoid(a) * b).astype(BF16)
        y = jnp.dot(hmid, w2_ref[...], preferred_element_type=F32) * gate_ref[...]
        _store_row_tiled(ybuf, y, (slot,))
        scatter(i, slot)

    @pl.when(i == nb - 1)
    def _():
        @pl.when(i >= 1)
        def _():
            scatter_wait(i - 1, 1 - slot)
        scatter_wait(i, slot)


def _experts(n_tok, block_exp, n_valid, dst_row, h_rt, slot_gate, w1, w3, w2):
    n_blocks = block_exp.shape[0]
    buf = pltpu.VMEM((2, MOE_BLOCK * ROW_TILES, LANES), F32)
    grid_spec = pltpu.PrefetchScalarGridSpec(
        num_scalar_prefetch=3,
        grid=(n_blocks,),
        in_specs=[pl.BlockSpec(memory_space=pl.ANY),
                  pl.BlockSpec((MOE_BLOCK, 1), lambda i, be, nv, dst: (i, 0)),
                  pl.BlockSpec((None, D_MODEL, D_EXPERT), lambda i, be, nv, dst: (be[i], 0, 0)),
                  pl.BlockSpec((None, D_MODEL, D_EXPERT), lambda i, be, nv, dst: (be[i], 0, 0)),
                  pl.BlockSpec((None, D_EXPERT, D_MODEL), lambda i, be, nv, dst: (be[i], 0, 0))],
        out_specs=pl.BlockSpec(memory_space=pl.ANY),
        scratch_shapes=[buf, buf, pltpu.SemaphoreType.DMA((2,)), pltpu.SemaphoreType.DMA((2,))],
    )
    return pl.pallas_call(
        functools.partial(_expert_body, n_tok),
        grid_spec=grid_spec,
        out_shape=jax.ShapeDtypeStruct((2 * n_tok * ROW_TILES, LANES), F32),
        compiler_params=_cparams("arbitrary"),
        name="experts",
    )(block_exp, n_valid, dst_row, h_rt, slot_gate, w1, w3, w2)


def _route(logits):
    g_logits = logits[:, :N_GROUPS]
    grp = jnp.argmax(g_logits, axis=-1).astype(jnp.int32)
    p_grp = jnp.take_along_axis(jax.nn.softmax(g_logits, axis=-1), grp[:, None], axis=1)
    e_logits = logits[:, N_GROUPS:].reshape(-1, N_GROUPS, EXPERTS_PER_GROUP)
    e_logits = jnp.take_along_axis(e_logits, grp[:, None, None], axis=1)[:, 0]
    top_v, top_i = lax.top_k(e_logits, 2)
    gate = p_grp * jax.nn.softmax(top_v, axis=-1)
    eid = grp[:, None] * EXPERTS_PER_GROUP + top_i.astype(jnp.int32)
    return eid, gate


def _moe(h_rt, logits, w1, w3, w2):
    n_tok = logits.shape[0]
    eid, gate = _route(logits)
    n_asg = n_tok * 2
    flat_e = eid.reshape(n_asg)
    order = jnp.argsort(flat_e)
    se = flat_e[order]
    counts = jnp.bincount(flat_e, length=N_EXPERTS)
    padded = (counts + MOE_BLOCK - 1) // MOE_BLOCK * MOE_BLOCK
    ends_p = jnp.cumsum(padded)
    run_start = ends_p - padded
    dest = run_start[se] + jnp.arange(n_asg) - (jnp.cumsum(counts) - counts)[se]
    n_blocks = -(-n_asg // MOE_BLOCK) + N_EXPERTS
    n_slots = n_blocks * MOE_BLOCK
    asg_row = (order % 2) * n_tok + order // 2
    dst_row = jnp.zeros((n_slots,), jnp.int32).at[dest].set(asg_row.astype(jnp.int32))
    slot_gate = jnp.zeros((n_slots,), F32).at[dest].set(gate.reshape(n_asg)[order])
    block_start = jnp.arange(n_blocks) * MOE_BLOCK
    block_exp = jnp.minimum(jnp.searchsorted(ends_p, block_start, side='right'), N_EXPERTS - 1).astype(jnp.int32)
    n_valid = jnp.clip(counts[block_exp] - (block_start - run_start[block_exp]), 0, MOE_BLOCK).astype(jnp.int32)
    return _experts(n_tok, block_exp, n_valid, dst_row, h_rt, slot_gate[:, None], w1, w3, w2)


def _final_body(x_ref, ya_ref, yb_ref, g_ref, o_ref):
    rows = x_ref.shape[0]
    x = x_ref[...] + (_load_row_tiled(ya_ref, rows) + _load_row_tiled(yb_ref, rows))
    o_ref[...] = _rms_scale(x) * g_ref[...]


def _final_norm(x, y2, g, tm):
    n = x.shape[0]
    nt = n // tm
    row = pl.BlockSpec((tm, D_MODEL), lambda i: (i, 0))
    return pl.pallas_call(
        _final_body,
        grid=(nt,),
        in_specs=[row,
                  pl.BlockSpec((tm * ROW_TILES, LANES), lambda i: (i, 0)),
                  pl.BlockSpec((tm * ROW_TILES, LANES), lambda i: (i + nt, 0)),
                  pl.BlockSpec(g.shape, lambda i: (0, 0))],
        out_specs=row,
        out_shape=jax.ShapeDtypeStruct((n, D_MODEL), F32),
        compiler_params=_cparams("parallel"),
        name="final_norm",
    )(x, y2, y2, g)


def _row_tile(n):
    for tm in (768, 512, 384, 256, 128, 64, 32, 16, 8):
        if n % tm == 0:
            return tm
    raise ValueError(f"row count {n} is not a multiple of 8")


def kernel(x_prompt, x_sample, cache_k, cache_v, cache_logf, page_table, meta_tokens, norm_mix_g, w_in, b_forget,
           norm_sb_g, norm_fox_g, w_out, norm_ffn_g, w_group, b_group, w_router, b_router, w_exp_gate, w_exp_up,
           w_exp_down, norm_final_g):
    bsz, seq, _ = x_prompt.shape
    db, n_new, _ = x_sample.shape
    depth = w_in.shape[0]
    t = seq + N_META
    n_p = bsz * t
    n_s = db * n_new
    assert (seq % ATTN_BLOCK) == 0 and page_table.shape[1] % PAGES_PER_STEP == 0

    meta = jnp.broadcast_to(meta_tokens[None].astype(x_prompt.dtype), (bsz, N_META, D_MODEL))
    xp = jnp.concatenate([meta, x_prompt], axis=1).reshape(n_p, D_MODEL)
    xs = x_sample.reshape(n_s, D_MODEL)
    ck_rows = cache_k.reshape(-1, HEAD_DIM)
    cv_rows = cache_v.reshape(-1, HEAD_DIM)
    tm_p, tm_s = _row_tile(n_p), _row_tile(n_s)
    n_main = 3 * D_MODEL + W_FOX

    streams = {"p": (xp, None), "s": (xs, None)}
    outs = {name: {"k": [], "v": [], "lf": []} for name in streams}
    for l in range(depth):
        w_main = w_in[l, :, :n_main].astype(BF16)
        w_f = w_in[l, :, n_main:].astype(BF16)
        b_f = b_forget[l][None, :]
        g_mix = norm_mix_g[l][None, :]
        w_o = w_out[l].astype(BF16)
        w_route = jnp.concatenate([w_group[l], w_router[l]], axis=1)
        b_route = jnp.concatenate([b_group[l], b_router[l]])[None, :]
        w1 = w_exp_gate[l].astype(BF16)
        w3 = w_exp_up[l].astype(BF16)
        w2 = w_exp_down[l].astype(BF16)
        new_streams = {}
        for name, tm in (("p", tm_p), ("s", tm_s)):
            x, y2 = streams[name]
            r = _inproj(x, y2, g_mix, w_main, w_f, b_f, tm)
            if y2 is not None:
                x, r = r[0], r[1:]
            q, k, v, gate, logf = r
            outs[name]["k"].append(k)
            outs[name]["v"].append(v)
            outs[name]["lf"].append(logf)
            if name == "p":
                q3, k3, v3 = (a.reshape(bsz, t, D_MODEL) for a in (q, k, v))
                ccol, crow_meta, crow = _forget_cumsum(logf.reshape(bsz, t, H_FOX))
                o_sb, o_fx = _prompt_attn(q3, k3, v3, ccol, crow_meta, crow)
                o_parts = (o_sb.reshape(n_p, W_SB), o_fx.reshape(n_p, W_FOX))
            else:
                o_parts = (_sample_attn(l, page_table, q, k, v, logf, ck_rows, cv_rows, cache_logf, n_new),)
            x_mid, h_rt, logits = _merge(x, o_parts, gate, norm_sb_g[l][None, :], norm_fox_g[l][None, :], w_o,
                                         norm_ffn_g[l][None, :], w_route, b_route, tm)
            new_streams[name] = (x_mid, _moe(h_rt, logits, w1, w3, w2))
        streams = new_streams

    g_fin = norm_final_g[None, :]
    y_prompt = _final_norm(*streams["p"], g_fin, tm_p).reshape(bsz, t, D_MODEL)[:, N_META:]
    y_sample = _final_norm(*streams["s"], g_fin, tm_s).reshape(db, n_new, D_MODEL)
    heads_p = (bsz, t, N_HEADS, HEAD_DIM)
    heads_s = (db, n_new, N_HEADS, HEAD_DIM)
    return (y_prompt, y_sample,
            jnp.stack([a.reshape(heads_p) for a in outs["p"]["k"]]),
            jnp.stack([a.reshape(heads_p) for a in outs["p"]["v"]]),
            jnp.stack([a.reshape(bsz, t, H_FOX) for a in outs["p"]["lf"]]),
            jnp.stack([a.reshape(heads_s) for a in outs["s"]["k"]]),
            jnp.stack([a.reshape(heads_s) for a in outs["s"]["v"]]),
            jnp.stack([a.reshape(db, n_new, H_FOX) for a in outs["s"]["lf"]]))
```

```python
import functools

import jax
import jax.numpy as jnp
from jax import lax
from jax.experimental import pallas as pl
from jax.experimental.pallas import tpu as pltpu

F32 = jnp.float32
BF16 = jnp.bfloat16

D_MODEL = 1024
HEAD_DIM = 64
H_SB = 8
H_FOX = 8
N_HEADS = H_SB + H_FOX
W_SB = H_SB * HEAD_DIM
W_FOX = H_FOX * HEAD_DIM
N_META = 16
N_GROUPS = 4
EXPERTS_PER_GROUP = 8
N_EXPERTS = N_GROUPS * EXPERTS_PER_GROUP
D_EXPERT = D_MODEL // 2
MOE_BLOCK = 256
RMS_EPS = 1e-6
PAGE_SIZE = 128
LANES = 128
SUBLANES = 8
ROW_TILES = D_MODEL // LANES
ATTN_BLOCK = 256
Q_TILE = 128
PAGES_PER_STEP = 4
NEG_BIG = -1e30
VMEM_LIMIT = 56 * 1024 * 1024


def _cparams(*sem):
    return pltpu.CompilerParams(dimension_semantics=sem, vmem_limit_bytes=VMEM_LIMIT)


def _log_sigmoid_fast(z):
    return jnp.minimum(z, 0.0) - jnp.log(1.0 + jnp.exp(-jnp.abs(z)))


def _log_sigmoid(x):
    return jnp.minimum(x, 0.0) - jnp.log1p(jnp.exp(-jnp.abs(x)))


def _rms_scale(x):
    return x * lax.rsqrt(jnp.mean(x * x, axis=-1, keepdims=True) + RMS_EPS)


def _dot_nt(a, b):
    return lax.dot_general(a, b, (((1,), (1,)), ((), ())), preferred_element_type=F32)


def _tri(n, kind):
    a = lax.broadcasted_iota(jnp.int32, (n, n), 0)
    b = lax.broadcasted_iota(jnp.int32, (n, n), 1)
    m = {"suffix": a > b, "prefix": b <= a, "eye": a == b}[kind]
    return jnp.where(m, 1.0, 0.0).astype(BF16)


def _pieces(x, parts):
    out = []
    rem = x
    for p in range(parts):
        piece = rem.astype(BF16)
        out.append(piece)
        if p + 1 < parts:
            rem = rem - piece.astype(F32)
    return out


def _split_dot(x, m01, parts):
    return sum(jnp.dot(p, m01, preferred_element_type=F32) for p in _pieces(x, parts))


def _split_dot_rhs(m01, x, parts):
    return sum(jnp.dot(m01, p, preferred_element_type=F32) for p in _pieces(x, parts))


def _transpose_small(x, parts=3):
    eye = _tri(x.shape[1], "eye")
    return sum(_dot_nt(eye, p) for p in _pieces(x, parts))


def _load_row_tiled(ref, rows):
    return jnp.concatenate([ref[pl.ds(s, rows, stride=ROW_TILES), :] for s in range(ROW_TILES)], axis=1)


def _store_row_tiled(ref, x, lead=()):
    rows = x.shape[0]
    for s in range(ROW_TILES):
        ref[lead + (pl.ds(s, rows, stride=ROW_TILES), slice(None))] = x[:, s * LANES:(s + 1) * LANES]


def _inproj_body(has_res, *refs):
    if has_res:
        x_ref, ya_ref, yb_ref, g_ref, w_ref, wf_ref, bf_ref, xo_ref, q_ref, k_ref, v_ref, gate_ref, logf_ref = refs
        rows = x_ref.shape[0]
        x = x_ref[...] + (_load_row_tiled(ya_ref, rows) + _load_row_tiled(yb_ref, rows))
        xo_ref[...] = x
    else:
        x_ref, g_ref, w_ref, wf_ref, bf_ref, q_ref, k_ref, v_ref, gate_ref, logf_ref = refs
        x = x_ref[...]
    h = (_rms_scale(x) * g_ref[...]).astype(BF16)
    half = D_MODEL // 2
    q_scale = HEAD_DIM ** -0.5
    for c in range(7):
        z = jnp.dot(h, w_ref[:, c * half:(c + 1) * half], preferred_element_type=F32)
        dst = pl.ds((c % 2) * half, half)
        if c < 2:
            q_ref[:, dst] = (z * q_scale).astype(BF16)
        elif c < 4:
            k_ref[:, dst] = z
        elif c < 6:
            v_ref[:, dst] = z
        else:
            gate_ref[...] = z.astype(BF16)
    zf = jnp.dot(h, wf_ref[...], preferred_element_type=F32) + bf_ref[...]
    logf_ref[...] = _log_sigmoid(zf)


def _inproj(x, y2, g, w_main, w_f, b_f, tm):
    n = x.shape[0]
    nt = n // tm
    row = lambda c: pl.BlockSpec((tm, c), lambda i: (i, 0))
    full = lambda a: pl.BlockSpec(a.shape, lambda i: (0,) * a.ndim)
    out_shape = [jax.ShapeDtypeStruct((n, D_MODEL), BF16),
                 jax.ShapeDtypeStruct((n, D_MODEL), F32),
                 jax.ShapeDtypeStruct((n, D_MODEL), F32),
                 jax.ShapeDtypeStruct((n, W_FOX), BF16),
                 jax.ShapeDtypeStruct((n, H_FOX), F32)]
    out_specs = [row(D_MODEL), row(D_MODEL), row(D_MODEL), row(W_FOX), row(H_FOX)]
    in_specs = [row(D_MODEL)]
    args = [x]
    if y2 is not None:
        out_shape = [jax.ShapeDtypeStruct((n, D_MODEL), F32)] + out_shape
        out_specs = [row(D_MODEL)] + out_specs
        in_specs += [pl.BlockSpec((tm * ROW_TILES, LANES), lambda i: (i, 0)),
                     pl.BlockSpec((tm * ROW_TILES, LANES), lambda i: (i + nt, 0))]
        args += [y2, y2]
    return pl.pallas_call(
        functools.partial(_inproj_body, y2 is not None),
        grid=(nt,),
        in_specs=in_specs + [full(g), full(w_main), full(w_f), full(b_f)],
        out_specs=out_specs,
        out_shape=out_shape,
        compiler_params=_cparams("parallel"),
        name="inproj",
    )(*args, g, w_main, w_f, b_f)


def _forget_cumsum_body(lf_ref, ccol_ref, crow_meta_ref, crow_ref):
    t = lf_ref.shape[1]
    n_blocks = (t - N_META) // ATTN_BLOCK
    meta = pl.ds(0, N_META)
    c = _split_dot_rhs(_tri(N_META, "prefix"), lf_ref[0, meta, :], 3)
    ccol_ref[0, meta, :] = c
    crow_meta_ref[0] = _transpose_small(c)
    carry0 = c[N_META - 1:N_META, :]
    prefix = _tri(ATTN_BLOCK, "prefix")

    def block(i, carry):
        r0 = pl.multiple_of(N_META + i * ATTN_BLOCK, 16)
        rows = pl.ds(r0, ATTN_BLOCK)
        c = _split_dot_rhs(prefix, lf_ref[0, rows, :], 3) + carry
        ccol_ref[0, rows, :] = c
        crow_ref[0, :, pl.ds(pl.multiple_of(i * ATTN_BLOCK, ATTN_BLOCK), ATTN_BLOCK)] = _transpose_small(c)
        return c[ATTN_BLOCK - 1:ATTN_BLOCK, :]

    lax.fori_loop(0, n_blocks, block, carry0)


def _forget_cumsum(logf):
    b, t, _ = logf.shape
    return pl.pallas_call(
        _forget_cumsum_body,
        grid=(b,),
        in_specs=[pl.BlockSpec((1, t, H_FOX), lambda i: (i, 0, 0))],
        out_specs=[pl.BlockSpec((1, t, H_FOX), lambda i: (i, 0, 0)),
                   pl.BlockSpec((1, H_FOX, N_META), lambda i: (i, 0, 0)),
                   pl.BlockSpec((1, H_FOX, t - N_META), lambda i: (i, 0, 0))],
        out_shape=[jax.ShapeDtypeStruct((b, t, H_FOX), F32),
                   jax.ShapeDtypeStruct((b, H_FOX, N_META), F32),
                   jax.ShapeDtypeStruct((b, H_FOX, t - N_META), F32)],
        compiler_params=_cparams("parallel"),
        name="forget_cumsum",
    )(logf)


def _sb_tile(q, k, v, tail, diag, off=0):
    tq, tk = q.shape[0], k.shape[0]
    z = _dot_nt(q, k)
    ls = _log_sigmoid_fast(z)
    l1m = ls - z
    if diag:
        row = lax.broadcasted_iota(jnp.int32, (tq, tk), 0)
        col = lax.broadcasted_iota(jnp.int32, (tq, tk), 1)
        valid = col < row + off
        l1m = jnp.where(valid, l1m, 0.0)
    suffix = _split_dot(l1m, _tri(tk, "suffix"), 2)
    w = jnp.exp(ls + suffix + tail)
    if diag:
        w = jnp.where(valid, w, 0.0)
    o = jnp.dot(w.astype(BF16), v, preferred_element_type=F32)
    tail = tail + suffix[:, 0:1] + l1m[:, 0:1]
    return o, tail


def _fx_tile(q, k, v, cq, ck, m, l, acc, diag, off=0):
    tq, tk = q.shape[0], k.shape[0]
    z = _dot_nt(q, k) + (cq - ck)
    if diag:
        row = lax.broadcasted_iota(jnp.int32, (tq, tk), 0)
        col = lax.broadcasted_iota(jnp.int32, (tq, tk), 1)
        z = jnp.where(col <= row + off, z, -jnp.inf)
    m_new = jnp.maximum(m, jnp.max(z, axis=-1, keepdims=True))
    alpha = jnp.exp(m - m_new)
    p = jnp.exp(z - m_new)
    l = alpha * l + jnp.sum(p, axis=-1, keepdims=True)
    acc = alpha * acc + jnp.dot(p.astype(BF16), v, preferred_element_type=F32)
    return m_new, l, acc


def _prompt_attn_body(qsb_ref, ksb_ref, vsb_ref, qfx_ref, kfx_ref, vfx_ref, ccol_ref, crm_ref, crr_ref,
                      osb_ref, ofx_ref, qs, ks, vs, acc, col):
    for g, (qr, kr, vr) in enumerate(((qsb_ref, ksb_ref, vsb_ref), (qfx_ref, kfx_ref, vfx_ref))):
        for hh in range(2):
            lanes = slice(hh * HEAD_DIM, (hh + 1) * HEAD_DIM)
            qs[2 * g + hh] = qr[0, :, lanes]
            ks[2 * g + hh] = kr[0, :, lanes].astype(BF16)
            vs[2 * g + hh] = vr[0, :, lanes].astype(BF16)
    t = qsb_ref.shape[1]
    n_blocks = (t - N_META) // Q_TILE
    pair = pl.program_id(1)
    meta = pl.ds(0, N_META)
    head_lane = lax.broadcasted_iota(jnp.int32, (1, H_FOX), 1)

    def query_bias(rows, hh):
        sel = head_lane == pair * 2 + hh
        return jnp.sum(jnp.where(sel, ccol_ref[0, rows, :], 0.0), axis=1, keepdims=True)

    def key_bias_meta(hh):
        return crm_ref[0, pl.ds(pair * 2 + hh, 1), :]

    def fx_init(tq):
        return (jnp.full((tq, 1), NEG_BIG, F32), jnp.zeros((tq, 1), F32), jnp.zeros((tq, HEAD_DIM), F32))

    for hh in range(2):
        lanes = slice(hh * HEAD_DIM, (hh + 1) * HEAD_DIM)
        o_meta, _ = _sb_tile(qs[hh, meta, :], ks[hh, meta, :], vs[hh, meta, :], jnp.zeros((N_META, 1), F32), True)
        osb_ref[0, meta, lanes] = o_meta
        g = 2 + hh
        _, l, a = _fx_tile(qs[g, meta, :], ks[g, meta, :], vs[g, meta, :], query_bias(meta, hh), key_bias_meta(hh),
                           *fx_init(N_META), True)
        ofx_ref[0, meta, lanes] = a / l

    def tiles(rows, keys, ck_of, diag, first, off=0):
        tq = Q_TILE
        tk = keys.size
        k = [ks[g, keys, :] for g in range(4)]
        z = [_dot_nt(qs[g, rows, :], k[g]) for g in range(4)]
        if diag:
            row = lax.broadcasted_iota(jnp.int32, (tq, tk), 0)
            col_id = lax.broadcasted_iota(jnp.int32, (tq, tk), 1)
            strict = col_id < row + off
            incl = col_id <= row + off
        ls = [_log_sigmoid_fast(z[hh]) for hh in range(2)]
        l1m = [ls[hh] - z[hh] for hh in range(2)]
        if diag:
            l1m = [jnp.where(strict, x, 0.0) for x in l1m]
        parts = [p for hh in range(2) for p in _pieces(l1m[hh], 2)]
        sums = jnp.dot(jnp.concatenate(parts, axis=0), _tri(tk, "suffix"), preferred_element_type=F32)
        suffix = [sums[(2 * hh) * tq:(2 * hh + 1) * tq] + sums[(2 * hh + 1) * tq:(2 * hh + 2) * tq]
                  for hh in range(2)]
        fx = []
        for hh in range(2):
            g = 2 + hh
            m_old, l_old, a_old = fx_init(tq) if first else (col[2 + hh], col[4 + hh], acc[g])
            zf = z[g] + (col[6 + hh] - ck_of(hh))
            if diag:
                zf = jnp.where(incl, zf, -jnp.inf)
            m_new = jnp.maximum(m_old, jnp.max(zf, axis=-1, keepdims=True))
            alpha = jnp.exp(m_old - m_new)
            p = jnp.exp(zf - m_new)
            col[2 + hh] = m_new
            col[4 + hh] = alpha * l_old + jnp.sum(p, axis=-1, keepdims=True)
            fx.append((alpha * a_old, p.astype(BF16)))
        for hh in range(2):
            g = 2 + hh
            acc[g] = fx[hh][0] + jnp.dot(fx[hh][1], vs[g, keys, :], preferred_element_type=F32)
        for hh in range(2):
            tail = jnp.zeros((tq, 1), F32) if first else col[hh]
            w = jnp.exp(ls[hh] + suffix[hh] + tail)
            if diag:
                w = jnp.where(strict, w, 0.0)
            o = jnp.dot(w.astype(BF16), vs[hh, keys, :], preferred_element_type=F32)
            acc[hh] = o if first else acc[hh] + o
            col[hh] = tail + suffix[hh][:, 0:1] + l1m[hh][:, 0:1]

    def q_block(qi, carry):
        r0 = pl.multiple_of(N_META + qi * Q_TILE, 16)
        rows = pl.ds(r0, Q_TILE)
        for hh in range(2):
            col[6 + hh] = query_bias(rows, hh)

        def key_tile(kj):
            return pl.ds(pl.multiple_of(N_META + kj * ATTN_BLOCK, 16), ATTN_BLOCK)

        def key_bias(kj):
            c0 = pl.multiple_of(kj * ATTN_BLOCK, ATTN_BLOCK)
            return lambda hh: crr_ref[0, pl.ds(pair * 2 + hh, 1), pl.ds(c0, ATTN_BLOCK)]

        kd = (qi * Q_TILE) // ATTN_BLOCK
        tiles(rows, key_tile(kd), key_bias(kd), True, True, qi * Q_TILE - kd * ATTN_BLOCK)

        def k_block(step, c):
            kj = kd - 1 - step
            tiles(rows, key_tile(kj), key_bias(kj), False, False)
            return c

        lax.fori_loop(0, kd, k_block, 0)
        tiles(rows, meta, key_bias_meta, False, False)
        for hh in range(2):
            lanes = slice(hh * HEAD_DIM, (hh + 1) * HEAD_DIM)
            osb_ref[0, rows, lanes] = acc[hh]
            ofx_ref[0, rows, lanes] = acc[2 + hh] / col[4 + hh]
        return carry

    lax.fori_loop(0, n_blocks, q_block, 0)


def _prompt_attn(q, k, v, ccol, crow_meta, crow):
    b, t, _ = q.shape
    n_pairs = H_SB // 2
    sb_col = pl.BlockSpec((1, t, LANES), lambda i, j: (i, 0, j))
    fx_col = pl.BlockSpec((1, t, LANES), lambda i, j: (i, 0, j + n_pairs))
    whole = lambda a: pl.BlockSpec((1,) + a.shape[1:], lambda i, j: (i, 0, 0))
    return pl.pallas_call(
        _prompt_attn_body,
        grid=(b, n_pairs),
        in_specs=[sb_col, sb_col, sb_col, fx_col, fx_col, fx_col, whole(ccol), whole(crow_meta), whole(crow)],
        out_specs=[sb_col, sb_col],
        out_shape=[jax.ShapeDtypeStruct((b, t, W_SB), F32), jax.ShapeDtypeStruct((b, t, W_FOX), F32)],
        scratch_shapes=[pltpu.VMEM((4, t, HEAD_DIM), BF16)] * 3
                       + [pltpu.VMEM((4, Q_TILE, HEAD_DIM), F32), pltpu.VMEM((8, Q_TILE, 1), F32)],
        compiler_params=_cparams("parallel", "parallel"),
        name="prompt_attn",
    )(q, k, v, q, k, v, ccol, crow_meta, crow)


def _sample_body(n_new, pt_ref, q_ref, kn_ref, vn_ref, lfn_ref, *refs):
    npg = PAGES_PER_STEP
    k_refs = refs[0:npg]
    v_refs = refs[npg:2 * npg]
    lf_refs = refs[2 * npg:3 * npg]
    o_ref = refs[3 * npg]
    q_scr, acc, m_scr, l_scr, tail_scr, rc_scr, cq_scr = refs[3 * npg + 1:]
    sb_rows = H_SB * n_new
    step = pl.program_id(1)
    tk = npg * PAGE_SIZE
    head_rows = lambda x, h: x[h * n_new:(h + 1) * n_new]
    head_lanes = lambda h: slice(h * HEAD_DIM, (h + 1) * HEAD_DIM)

    @pl.when(step == 0)
    def _():
        for h in range(N_HEADS):
            q_scr[h] = q_ref[:, head_lanes(h)].astype(F32)
        s_new = jnp.concatenate([_dot_nt(q_scr[h], kn_ref[:, head_lanes(h)]) for h in range(N_HEADS)], axis=0)
        qi = lax.broadcasted_iota(jnp.int32, (N_HEADS * n_new, n_new), 0) % n_new
        ki = lax.broadcasted_iota(jnp.int32, (N_HEADS * n_new, n_new), 1)
        z = s_new[:sb_rows]
        strict = (ki < qi)[:sb_rows]
        ls = _log_sigmoid_fast(z)
        l1m = jnp.where(strict, ls - z, 0.0)
        suffix = _split_dot(l1m, _tri(n_new, "suffix"), 3)
        w_sb = jnp.where(strict, jnp.exp(ls + suffix), 0.0)
        tail_scr[...] = jnp.sum(l1m, axis=-1, keepdims=True)
        cn = _split_dot_rhs(_tri(n_new, "prefix"), lfn_ref[...], 3)
        cn_t = _transpose_small(cn)
        cq = jnp.concatenate([cn[:, h:h + 1] for h in range(H_FOX)], axis=0)
        ck = jnp.concatenate([jnp.broadcast_to(cn_t[h:h + 1, :], (n_new, n_new)) for h in range(H_FOX)], axis=0)
        cq_scr[...] = cq
        zf = jnp.where((ki <= qi)[sb_rows:], s_new[sb_rows:] + (cq - ck), -jnp.inf)
        m0 = jnp.max(zf, axis=-1, keepdims=True)
        p = jnp.exp(zf - m0)
        m_scr[...] = m0
        l_scr[...] = jnp.sum(p, axis=-1, keepdims=True)
        rc_scr[...] = jnp.zeros_like(rc_scr)
        pw = jnp.concatenate([w_sb, p], axis=0)
        acc[...] = jnp.concatenate(
            [jnp.dot(head_rows(pw, h), vn_ref[:, head_lanes(h)], preferred_element_type=F32)
             for h in range(N_HEADS)], axis=0)

    def head_t(refs_, h):
        return jnp.concatenate([r[h].astype(BF16) for r in refs_], axis=1)

    s = jnp.concatenate([jnp.dot(q_scr[h].astype(BF16), head_t(k_refs, h), preferred_element_type=F32)
                         for h in range(N_HEADS)], axis=0)
    u = _tri(tk, "suffix")
    z = s[:sb_rows]
    ls = _log_sigmoid_fast(z)
    l1m = ls - z
    suffix = _split_dot(l1m, u, 2)
    tail = tail_scr[...]
    w_sb = jnp.exp(ls + suffix + tail)
    tail_scr[...] = tail + suffix[:, 0:1] + l1m[:, 0:1]
    plf_t = jnp.concatenate([lf_refs[j][...] for j in range(npg)], axis=1)
    rc_prev = rc_scr[...]
    rc = _split_dot(plf_t, u, 3) + rc_prev
    rc_scr[...] = rc_prev + jnp.sum(plf_t, axis=-1, keepdims=True)
    bias = jnp.concatenate([jnp.broadcast_to(rc[h:h + 1, :], (n_new, tk)) for h in range(H_FOX)], axis=0)
    zf = s[sb_rows:] + (cq_scr[...] + bias)
    m_prev = m_scr[...]
    m_new = jnp.maximum(m_prev, jnp.max(zf, axis=-1, keepdims=True))
    alpha = jnp.exp(m_prev - m_new)
    p = jnp.exp(zf - m_new)
    m_scr[...] = m_new
    l_scr[...] = alpha * l_scr[...] + jnp.sum(p, axis=-1, keepdims=True)
    pw = jnp.concatenate([w_sb, p], axis=0)
    scale = jnp.concatenate([jnp.ones((sb_rows, 1), F32), alpha], axis=0)
    pv = jnp.concatenate(
        [_dot_nt(head_rows(pw, h).astype(BF16), head_t(v_refs, h)) for h in range(N_HEADS)], axis=0)
    acc[...] = acc[...] * scale + pv

    @pl.when(step == pl.num_programs(1) - 1)
    def _():
        norm = jnp.concatenate([jnp.ones((sb_rows, 1), F32), 1.0 / l_scr[...]], axis=0)
        out = acc[...] * norm
        for h in range(N_HEADS):
            o_ref[:, head_lanes(h)] = head_rows(out, h)


def _sample_attn(layer, page_table, q, k_new, v_new, logf_new, cache_k, cache_v, cache_logf, n_new):
    db, n_pages = page_table.shape
    npg = PAGES_PER_STEP
    n_steps = n_pages // npg
    rows = N_HEADS * n_new
    fx_rows = H_FOX * n_new

    def page_of(b, s, pt, j):
        return pt[b, n_pages - (s + 1) * npg + j]

    per_row = lambda c: pl.BlockSpec((n_new, c), lambda b, s, pt: (b, 0))
    kv_specs = [pl.BlockSpec((None, None, N_HEADS, HEAD_DIM, PAGE_SIZE),
                             functools.partial(lambda b, s, pt, j: (layer, page_of(b, s, pt, j), 0, 0, 0), j=j))
                for j in range(npg)]
    lf_specs = [pl.BlockSpec((None, None, H_FOX, PAGE_SIZE),
                             functools.partial(lambda b, s, pt, j: (layer, page_of(b, s, pt, j), 0, 0), j=j))
                for j in range(npg)]
    grid_spec = pltpu.PrefetchScalarGridSpec(
        num_scalar_prefetch=1,
        grid=(db, n_steps),
        in_specs=[per_row(D_MODEL), per_row(D_MODEL), per_row(D_MODEL), per_row(H_FOX)]
                 + kv_specs + kv_specs + lf_specs,
        out_specs=per_row(D_MODEL),
        scratch_shapes=[pltpu.VMEM((N_HEADS, n_new, HEAD_DIM), F32),
                        pltpu.VMEM((rows, HEAD_DIM), F32),
                        pltpu.VMEM((fx_rows, 1), F32),
                        pltpu.VMEM((fx_rows, 1), F32),
                        pltpu.VMEM((rows - fx_rows, 1), F32),
                        pltpu.VMEM((H_FOX, 1), F32),
                        pltpu.VMEM((fx_rows, 1), F32)],
    )
    return pl.pallas_call(
        functools.partial(_sample_body, n_new),
        grid_spec=grid_spec,
        out_shape=jax.ShapeDtypeStruct((db * n_new, D_MODEL), F32),
        compiler_params=_cparams("parallel", "arbitrary"),
        name="sample_attn",
    )(page_table, q, k_new, v_new, logf_new, *([cache_k] * npg), *([cache_v] * npg), *([cache_logf] * npg))


def _merge_body(x_ref, o_ref, gate_ref, gsb_ref, gfx_ref, wo_ref, gffn_ref, wr_ref, br_ref,
                xmid_ref, h_ref, logit_ref):
    a_sb = (_rms_scale(o_ref[:, :W_SB]) * gsb_ref[...]).astype(BF16)
    a_fx = (_rms_scale(o_ref[:, W_SB:]) * gfx_ref[...] * jax.nn.sigmoid(gate_ref[...].astype(F32))).astype(BF16)
    y = (jnp.dot(a_sb, wo_ref[:W_SB, :], preferred_element_type=F32)
         + jnp.dot(a_fx, wo_ref[W_SB:, :], preferred_element_type=F32))
    x = x_ref[...] + y
    xmid_ref[...] = x
    h = _rms_scale(x) * gffn_ref[...]
    _store_row_tiled(h_ref, h)
    logit_ref[...] = jnp.dot(h, wr_ref[...], preferred_element_type=F32,
                             precision=lax.Precision.HIGHEST) + br_ref[...]


def _merge(x, o_parts, gate, g_sb, g_fx, w_out, g_ffn, w_route, b_route, tm):
    n = x.shape[0]
    row = lambda c: pl.BlockSpec((tm, c), lambda i: (i, 0))
    full = lambda a: pl.BlockSpec(a.shape, lambda i: (0,) * a.ndim)
    n_logit = w_route.shape[1]
    body = _merge_body
    if len(o_parts) == 2:
        def body(x_ref, osb_ref, ofx_ref, *rest):
            return _merge_body(x_ref, _Halves(osb_ref, ofx_ref), *rest)
    return pl.pallas_call(
        body,
        grid=(n // tm,),
        in_specs=[row(D_MODEL)] + [row(o.shape[1]) for o in o_parts]
                 + [row(W_FOX), full(g_sb), full(g_fx), full(w_out), full(g_ffn), full(w_route), full(b_route)],
        out_specs=[row(D_MODEL), pl.BlockSpec((tm * ROW_TILES, LANES), lambda i: (i, 0)), row(n_logit)],
        out_shape=[jax.ShapeDtypeStruct((n, D_MODEL), F32),
                   jax.ShapeDtypeStruct((n * ROW_TILES, LANES), F32),
                   jax.ShapeDtypeStruct((n, n_logit), F32)],
        compiler_params=_cparams("parallel"),
        name="merge",
    )(x, *o_parts, gate, g_sb, g_fx, w_out, g_ffn, w_route, b_route)


class _Halves:
    def __init__(self, lo, hi):
        self.lo, self.hi = lo, hi

    def __getitem__(self, idx):
        rows, cols = idx
        if cols == slice(None, W_SB):
            return self.lo[rows, :]
        assert cols == slice(W_SB, None)
        return self.hi[rows, :]


def _expert_body(n_tok, be_ref, nv_ref, dst_ref, h_hbm, gate_ref, w1_ref, w3_ref, w2_ref, y_hbm,
                 xbuf, ybuf, gsem, ssem):
    i = pl.program_id(0)
    nb = pl.num_programs(0)
    slot = i % 2
    tile = lambda r: pl.ds(pl.multiple_of(r * ROW_TILES, ROW_TILES), ROW_TILES)

    def gather(b, sl):
        def one(r, c):
            d = dst_ref[b * MOE_BLOCK + r]
            tok = jnp.where(d >= n_tok, d - n_tok, d)
            pltpu.make_async_copy(h_hbm.at[tile(tok), :], xbuf.at[sl, tile(r), :], gsem.at[sl]).start()
            return c
        lax.fori_loop(0, nv_ref[b], one, 0)

    def gather_wait(b, sl):
        n = nv_ref[b] * ROW_TILES

        @pl.when(n > 0)
        def _():
            pltpu.make_async_copy(h_hbm.at[pl.ds(0, n), :], xbuf.at[sl, pl.ds(0, n), :], gsem.at[sl]).wait()

    def scatter(b, sl):
        def one(r, c):
            d = dst_ref[b * MOE_BLOCK + r]
            pltpu.make_async_copy(ybuf.at[sl, tile(r), :], y_hbm.at[tile(d), :], ssem.at[sl]).start()
            return c
        lax.fori_loop(0, nv_ref[b], one, 0)

    def scatter_wait(b, sl):
        n = nv_ref[b] * ROW_TILES

        @pl.when(n > 0)
        def _():
            pltpu.make_async_copy(ybuf.at[sl, pl.ds(0, n), :], y_hbm.at[pl.ds(0, n), :], ssem.at[sl]).wait()

    @pl.when(i == 0)
    def _():
        xbuf[...] = jnp.zeros_like(xbuf)
        gather(0, 0)

    @pl.when(i + 1 < nb)
    def _():
        gather(i + 1, 1 - slot)

    gather_wait(i, slot)

    @pl.when(i >= 2)
    def _():
        scatter_wait(i - 2, slot)

    @pl.when(nv_ref[i] > 0)
    def _():
        x = _load_row_tiled(xbuf.at[slot], MOE_BLOCK).astype(BF16)
        a = jnp.dot(x, w1_ref[...], preferred_element_type=F32)
        b = jnp.dot(x, w3_ref[...], preferred_element_type=F32)
        hmid = (a * jax.nn.sigmoid(a) * b).astype(BF16)
        y = jnp.dot(hmid, w2_ref[...], preferred_element_type=F32) * gate_ref[...]
        _store_row_tiled(ybuf, y, (slot,))
        scatter(i, slot)

    @pl.when(i == nb - 1)
    def _():
        @pl.when(i >= 1)
        def _():
            scatter_wait(i - 1, 1 - slot)
        scatter_wait(i, slot)


def _experts(n_tok, block_exp, n_valid, dst_row, h_rt, slot_gate, w1, w3, w2):
    n_blocks = block_exp.shape[0]
    buf = pltpu.VMEM((2, MOE_BLOCK * ROW_TILES, LANES), F32)
    grid_spec = pltpu.PrefetchScalarGridSpec(
        num_scalar_prefetch=3,
        grid=(n_blocks,),
        in_specs=[pl.BlockSpec(memory_space=pl.ANY),
                  pl.BlockSpec((MOE_BLOCK, 1), lambda i, be, nv, dst: (i, 0)),
                  pl.BlockSpec((None, D_MODEL, D_EXPERT), lambda i, be, nv, dst: (be[i], 0, 0)),
                  pl.BlockSpec((None, D_MODEL, D_EXPERT), lambda i, be, nv, dst: (be[i], 0, 0)),
                  pl.BlockSpec((None, D_EXPERT, D_MODEL), lambda i, be, nv, dst: (be[i], 0, 0))],
        out_specs=pl.BlockSpec(memory_space=pl.ANY),
        scratch_shapes=[buf, buf, pltpu.SemaphoreType.DMA((2,)), pltpu.SemaphoreType.DMA((2,))],
    )
    return pl.pallas_call(
        functools.partial(_expert_body, n_tok),
        grid_spec=grid_spec,
        out_shape=jax.ShapeDtypeStruct((2 * n_tok * ROW_TILES, LANES), F32),
        compiler_params=_cparams("arbitrary"),
        name="experts",
    )(block_exp, n_valid, dst_row, h_rt, slot_gate, w1, w3, w2)


def _route(logits):
    g_logits = logits[:, :N_GROUPS]
    grp = jnp.argmax(g_logits, axis=-1).astype(jnp.int32)
    p_grp = jnp.take_along_axis(jax.nn.softmax(g_logits, axis=-1), grp[:, None], axis=1)
    e_logits = logits[:, N_GROUPS:].reshape(-1, N_GROUPS, EXPERTS_PER_GROUP)
    e_logits = jnp.take_along_axis(e_logits, grp[:, None, None], axis=1)[:, 0]
    top_v, top_i = lax.top_k(e_logits, 2)
    gate = p_grp * jax.nn.softmax(top_v, axis=-1)
    eid = grp[:, None] * EXPERTS_PER_GROUP + top_i.astype(jnp.int32)
    return eid, gate


def _moe(h_rt, logits, w1, w3, w2):
    n_tok = logits.shape[0]
    eid, gate = _route(logits)
    n_asg = n_tok * 2
    flat_e = eid.reshape(n_asg)
    order = jnp.argsort(flat_e)
    se = flat_e[order]
    counts = jnp.bincount(flat_e, length=N_EXPERTS)
    padded = (counts + MOE_BLOCK - 1) // MOE_BLOCK * MOE_BLOCK
    ends_p = jnp.cumsum(padded)
    run_start = ends_p - padded
    dest = run_start[se] + jnp.arange(n_asg) - (jnp.cumsum(counts) - counts)[se]
    n_blocks = -(-n_asg // MOE_BLOCK) + N_EXPERTS
    n_slots = n_blocks * MOE_BLOCK
    asg_row = (order % 2) * n_tok + order // 2
    dst_row = jnp.zeros((n_slots,), jnp.int32).at[dest].set(asg_row.astype(jnp.int32))
    slot_gate = jnp.zeros((n_slots,), F32).at[dest].set(gate.reshape(n_asg)[order])
    block_start = jnp.arange(n_blocks) * MOE_BLOCK
    block_exp = jnp.minimum(jnp.searchsorted(ends_p, block_start, side='right'), N_EXPERTS - 1).astype(jnp.int32)
    n_valid = jnp.clip(counts[block_exp] - (block_start - run_start[block_exp]), 0, MOE_BLOCK).astype(jnp.int32)
    return _experts(n_tok, block_exp, n_valid, dst_row, h_rt, slot_gate[:, None], w1, w3, w2)


def _final_body(x_ref, ya_ref, yb_ref, g_ref, o_ref):
    rows = x_ref.shape[0]
    x = x_ref[...] + (_load_row_tiled(ya_ref, rows) + _load_row_tiled(yb_ref, rows))
    o_ref[...] = _rms_scale(x) * g_ref[...]


def _final_norm(x, y2, g, tm):
    n = x.shape[0]
    nt = n // tm
    row = pl.BlockSpec((tm, D_MODEL), lambda i: (i, 0))
    return pl.pallas_call(
        _final_body,
        grid=(nt,),
        in_specs=[row,
                  pl.BlockSpec((tm * ROW_TILES, LANES), lambda i: (i, 0)),
                  pl.BlockSpec((tm * ROW_TILES, LANES), lambda i: (i + nt, 0)),
                  pl.BlockSpec(g.shape, lambda i: (0, 0))],
        out_specs=row,
        out_shape=jax.ShapeDtypeStruct((n, D_MODEL), F32),
        compiler_params=_cparams("parallel"),
        name="final_norm",
    )(x, y2, y2, g)


def _row_tile(n):
    for tm in (768, 512, 384, 256, 128, 64, 32, 16, 8):
        if n % tm == 0:
            return tm
    raise ValueError(f"row count {n} is not a multiple of 8")


def kernel(x_prompt, x_sample, cache_k, cache_v, cache_logf, page_table, meta_tokens, norm_mix_g, w_in, b_forget,
           norm_sb_g, norm_fox_g, w_out, norm_ffn_g, w_group, b_group, w_router, b_router, w_exp_gate, w_exp_up,
           w_exp_down, norm_final_g):
    bsz, seq, _ = x_prompt.shape
    db, n_new, _ = x_sample.shape
    depth = w_in.shape[0]
    t = seq + N_META
    n_p = bsz * t
    n_s = db * n_new
    assert (seq % ATTN_BLOCK) == 0 and page_table.shape[1] % PAGES_PER_STEP == 0

    meta = jnp.broadcast_to(meta_tokens[None].astype(x_prompt.dtype), (bsz, N_META, D_MODEL))
    xp = jnp.concatenate([meta, x_prompt], axis=1).reshape(n_p, D_MODEL)
    xs = x_sample.reshape(n_s, D_MODEL)
    ck_t = jnp.transpose(cache_k, (0, 1, 3, 4, 2))
    cv_t = jnp.transpose(cache_v, (0, 1, 3, 4, 2))
    clf_t = jnp.transpose(cache_logf, (0, 1, 3, 2))
    tm_p, tm_s = _row_tile(n_p), _row_tile(n_s)
    n_main = 3 * D_MODEL + W_FOX

    streams = {"p": (xp, None), "s": (xs, None)}
    outs = {name: {"k": [], "v": [], "lf": []} for name in streams}
    for l in range(depth):
        w_main = w_in[l, :, :n_main].astype(BF16)
        w_f = w_in[l, :, n_main:].astype(BF16)
        b_f = b_forget[l][None, :]
        g_mix = norm_mix_g[l][None, :]
        w_o = w_out[l].astype(BF16)
        w_route = jnp.concatenate([w_group[l], w_router[l]], axis=1)
        b_route = jnp.concatenate([b_group[l], b_router[l]])[None, :]
        w1 = w_exp_gate[l].astype(BF16)
        w3 = w_exp_up[l].astype(BF16)
        w2 = w_exp_down[l].astype(BF16)
        new_streams = {}
        for name, tm in (("p", tm_p), ("s", tm_s)):
            x, y2 = streams[name]
            r = _inproj(x, y2, g_mix, w_main, w_f, b_f, tm)
            if y2 is not None:
                x, r = r[0], r[1:]
            q, k, v, gate, logf = r
            outs[name]["k"].append(k)
            outs[name]["v"].append(v)
            outs[name]["lf"].append(logf)
            if name == "p":
                q3, k3, v3 = (a.reshape(bsz, t, D_MODEL) for a in (q, k, v))
                ccol, crow_meta, crow = _forget_cumsum(logf.reshape(bsz, t, H_FOX))
                o_sb, o_fx = _prompt_attn(q3, k3, v3, ccol, crow_meta, crow)
                o_parts = (o_sb.reshape(n_p, W_SB), o_fx.reshape(n_p, W_FOX))
            else:
                o_parts = (_sample_attn(l, page_table, q, k, v, logf, ck_t, cv_t, clf_t, n_new),)
            x_mid, h_rt, logits = _merge(x, o_parts, gate, norm_sb_g[l][None, :], norm_fox_g[l][None, :], w_o,
                                         norm_ffn_g[l][None, :], w_route, b_route, tm)
            new_streams[name] = (x_mid, _moe(h_rt, logits, w1, w3, w2))
        streams = new_streams

    g_fin = norm_final_g[None, :]
    y_prompt = _final_norm(*streams["p"], g_fin, tm_p).reshape(bsz, t, D_MODEL)[:, N_META:]
    y_sample = _final_norm(*streams["s"], g_fin, tm_s).reshape(db, n_new, D_MODEL)
    heads_p = (bsz, t, N_HEADS, HEAD_DIM)
    heads_s = (db, n_new, N_HEADS, HEAD_DIM)
    return (y_prompt, y_sample,
            jnp.stack([a.reshape(heads_p) for a in outs["p"]["k"]]),
            jnp.stack([a.reshape(heads_p) for a in outs["p"]["v"]]),
            jnp.stack([a.reshape(bsz, t, H_FOX) for a in outs["p"]["lf"]]),
            jnp.stack([a.reshape(heads_s) for a in outs["s"]["k"]]),
            jnp.stack([a.reshape(heads_s) for a in outs["s"]["v"]]),
            jnp.stack([a.reshape(db, n_new, H_FOX) for a in outs["s"]["lf"]]))
```

```python
import functools

import jax
import jax.numpy as jnp
from jax import lax
from jax.experimental import pallas as pl
from jax.experimental.pallas import tpu as pltpu

F32 = jnp.float32
BF16 = jnp.bfloat16

D_MODEL = 1024
HEAD_DIM = 64
H_SB = 8
H_FOX = 8
N_HEADS = H_SB + H_FOX
W_SB = H_SB * HEAD_DIM
W_FOX = H_FOX * HEAD_DIM
N_META = 16
N_GROUPS = 4
EXPERTS_PER_GROUP = 8
N_EXPERTS = N_GROUPS * EXPERTS_PER_GROUP
D_EXPERT = D_MODEL // 2
MOE_BLOCK = 256
RMS_EPS = 1e-6
PAGE_SIZE = 128
LANES = 128
SUBLANES = 8
ROW_TILES = D_MODEL // LANES
ATTN_BLOCK = 256
Q_TILE = 128
PAGES_PER_STEP = 8
NEG_BIG = -1e30
VMEM_LIMIT = 56 * 1024 * 1024


def _cparams(*sem):
    return pltpu.CompilerParams(dimension_semantics=sem, vmem_limit_bytes=VMEM_LIMIT)


def _log_sigmoid_fast(z):
    return jnp.minimum(z, 0.0) - jnp.log(1.0 + jnp.exp(-jnp.abs(z)))


def _log_sigmoid(x):
    return jnp.minimum(x, 0.0) - jnp.log1p(jnp.exp(-jnp.abs(x)))


def _rms_scale(x):
    return x * lax.rsqrt(jnp.mean(x * x, axis=-1, keepdims=True) + RMS_EPS)


def _dot_nt(a, b):
    return lax.dot_general(a, b, (((1,), (1,)), ((), ())), preferred_element_type=F32)


def _tri(n, kind):
    a = lax.broadcasted_iota(jnp.int32, (n, n), 0)
    b = lax.broadcasted_iota(jnp.int32, (n, n), 1)
    m = {"suffix": a > b, "prefix": b <= a, "eye": a == b}[kind]
    return jnp.where(m, 1.0, 0.0).astype(BF16)


def _pieces(x, parts):
    out = []
    rem = x
    for p in range(parts):
        piece = rem.astype(BF16)
        out.append(piece)
        if p + 1 < parts:
            rem = rem - piece.astype(F32)
    return out


def _split_dot(x, m01, parts):
    return sum(jnp.dot(p, m01, preferred_element_type=F32) for p in _pieces(x, parts))


def _split_dot_rhs(m01, x, parts):
    return sum(jnp.dot(m01, p, preferred_element_type=F32) for p in _pieces(x, parts))


def _transpose_small(x, parts=3):
    eye = _tri(x.shape[1], "eye")
    return sum(_dot_nt(eye, p) for p in _pieces(x, parts))


def _load_row_tiled(ref, rows):
    return jnp.concatenate([ref[pl.ds(s, rows, stride=ROW_TILES), :] for s in range(ROW_TILES)], axis=1)


def _store_row_tiled(ref, x, lead=()):
    rows = x.shape[0]
    for s in range(ROW_TILES):
        ref[lead + (pl.ds(s, rows, stride=ROW_TILES), slice(None))] = x[:, s * LANES:(s + 1) * LANES]


def _inproj_body(has_res, *refs):
    if has_res:
        x_ref, ya_ref, yb_ref, g_ref, w_ref, wf_ref, bf_ref, xo_ref, q_ref, k_ref, v_ref, gate_ref, logf_ref = refs
        rows = x_ref.shape[0]
        x = x_ref[...] + (_load_row_tiled(ya_ref, rows) + _load_row_tiled(yb_ref, rows))
        xo_ref[...] = x
    else:
        x_ref, g_ref, w_ref, wf_ref, bf_ref, q_ref, k_ref, v_ref, gate_ref, logf_ref = refs
        x = x_ref[...]
    h = (_rms_scale(x) * g_ref[...]).astype(BF16)
    half = D_MODEL // 2
    q_scale = HEAD_DIM ** -0.5
    for c in range(7):
        z = jnp.dot(h, w_ref[:, c * half:(c + 1) * half], preferred_element_type=F32)
        dst = pl.ds((c % 2) * half, half)
        if c < 2:
            q_ref[:, dst] = (z * q_scale).astype(BF16)
        elif c < 4:
            k_ref[:, dst] = z
        elif c < 6:
            v_ref[:, dst] = z
        else:
            gate_ref[...] = z.astype(BF16)
    zf = jnp.dot(h, wf_ref[...], preferred_element_type=F32) + bf_ref[...]
    logf_ref[...] = _log_sigmoid(zf)


def _inproj(x, y2, g, w_main, w_f, b_f, tm):
    n = x.shape[0]
    nt = n // tm
    row = lambda c: pl.BlockSpec((tm, c), lambda i: (i, 0))
    full = lambda a: pl.BlockSpec(a.shape, lambda i: (0,) * a.ndim)
    out_shape = [jax.ShapeDtypeStruct((n, D_MODEL), BF16),
                 jax.ShapeDtypeStruct((n, D_MODEL), F32),
                 jax.ShapeDtypeStruct((n, D_MODEL), F32),
                 jax.ShapeDtypeStruct((n, W_FOX), BF16),
                 jax.ShapeDtypeStruct((n, H_FOX), F32)]
    out_specs = [row(D_MODEL), row(D_MODEL), row(D_MODEL), row(W_FOX), row(H_FOX)]
    in_specs = [row(D_MODEL)]
    args = [x]
    if y2 is not None:
        out_shape = [jax.ShapeDtypeStruct((n, D_MODEL), F32)] + out_shape
        out_specs = [row(D_MODEL)] + out_specs
        in_specs += [pl.BlockSpec((tm * ROW_TILES, LANES), lambda i: (i, 0)),
                     pl.BlockSpec((tm * ROW_TILES, LANES), lambda i: (i + nt, 0))]
        args += [y2, y2]
    return pl.pallas_call(
        functools.partial(_inproj_body, y2 is not None),
        grid=(nt,),
        in_specs=in_specs + [full(g), full(w_main), full(w_f), full(b_f)],
        out_specs=out_specs,
        out_shape=out_shape,
        compiler_params=_cparams("parallel"),
        name="inproj",
    )(*args, g, w_main, w_f, b_f)


def _forget_cumsum_body(lf_ref, ccol_ref, crow_meta_ref, crow_ref):
    t = lf_ref.shape[1]
    n_blocks = (t - N_META) // ATTN_BLOCK
    meta = pl.ds(0, N_META)
    c = _split_dot_rhs(_tri(N_META, "prefix"), lf_ref[0, meta, :], 3)
    ccol_ref[0, meta, :] = c
    crow_meta_ref[0] = _transpose_small(c)
    carry0 = c[N_META - 1:N_META, :]
    prefix = _tri(ATTN_BLOCK, "prefix")

    def block(i, carry):
        r0 = pl.multiple_of(N_META + i * ATTN_BLOCK, 16)
        rows = pl.ds(r0, ATTN_BLOCK)
        c = _split_dot_rhs(prefix, lf_ref[0, rows, :], 3) + carry
        ccol_ref[0, rows, :] = c
        crow_ref[0, :, pl.ds(pl.multiple_of(i * ATTN_BLOCK, ATTN_BLOCK), ATTN_BLOCK)] = _transpose_small(c)
        return c[ATTN_BLOCK - 1:ATTN_BLOCK, :]

    lax.fori_loop(0, n_blocks, block, carry0)


def _forget_cumsum(logf):
    b, t, _ = logf.shape
    return pl.pallas_call(
        _forget_cumsum_body,
        grid=(b,),
        in_specs=[pl.BlockSpec((1, t, H_FOX), lambda i: (i, 0, 0))],
        out_specs=[pl.BlockSpec((1, t, H_FOX), lambda i: (i, 0, 0)),
                   pl.BlockSpec((1, H_FOX, N_META), lambda i: (i, 0, 0)),
                   pl.BlockSpec((1, H_FOX, t - N_META), lambda i: (i, 0, 0))],
        out_shape=[jax.ShapeDtypeStruct((b, t, H_FOX), F32),
                   jax.ShapeDtypeStruct((b, H_FOX, N_META), F32),
                   jax.ShapeDtypeStruct((b, H_FOX, t - N_META), F32)],
        compiler_params=_cparams("parallel"),
        name="forget_cumsum",
    )(logf)


def _sb_tile(q, k, v, tail, diag, off=0):
    tq, tk = q.shape[0], k.shape[0]
    z = _dot_nt(q, k)
    ls = _log_sigmoid_fast(z)
    l1m = ls - z
    if diag:
        row = lax.broadcasted_iota(jnp.int32, (tq, tk), 0)
        col = lax.broadcasted_iota(jnp.int32, (tq, tk), 1)
        valid = col < row + off
        l1m = jnp.where(valid, l1m, 0.0)
    suffix = _split_dot(l1m, _tri(tk, "suffix"), 2)
    w = jnp.exp(ls + suffix + tail)
    if diag:
        w = jnp.where(valid, w, 0.0)
    o = jnp.dot(w.astype(BF16), v, preferred_element_type=F32)
    tail = tail + suffix[:, 0:1] + l1m[:, 0:1]
    return o, tail


def _fx_tile(q, k, v, cq, ck, m, l, acc, diag, off=0):
    tq, tk = q.shape[0], k.shape[0]
    z = _dot_nt(q, k) + (cq - ck)
    if diag:
        row = lax.broadcasted_iota(jnp.int32, (tq, tk), 0)
        col = lax.broadcasted_iota(jnp.int32, (tq, tk), 1)
        z = jnp.where(col <= row + off, z, -jnp.inf)
    m_new = jnp.maximum(m, jnp.max(z, axis=-1, keepdims=True))
    alpha = jnp.exp(m - m_new)
    p = jnp.exp(z - m_new)
    l = alpha * l + jnp.sum(p, axis=-1, keepdims=True)
    acc = alpha * acc + jnp.dot(p.astype(BF16), v, preferred_element_type=F32)
    return m_new, l, acc


def _prompt_attn_body(qsb_ref, ksb_ref, vsb_ref, qfx_ref, kfx_ref, vfx_ref, ccol_ref, crm_ref, crr_ref,
                      osb_ref, ofx_ref, qs, ks, vs, acc, col, u_scr):
    u_scr[...] = _tri(ATTN_BLOCK, "suffix")
    for g, (qr, kr, vr) in enumerate(((qsb_ref, ksb_ref, vsb_ref), (qfx_ref, kfx_ref, vfx_ref))):
        for hh in range(2):
            lanes = slice(hh * HEAD_DIM, (hh + 1) * HEAD_DIM)
            qs[2 * g + hh] = qr[0, :, lanes]
            ks[2 * g + hh] = kr[0, :, lanes].astype(BF16)
            vs[2 * g + hh] = vr[0, :, lanes].astype(BF16)
    t = qsb_ref.shape[1]
    n_blocks = (t - N_META) // Q_TILE
    pair = pl.program_id(1)
    meta = pl.ds(0, N_META)
    head_lane = lax.broadcasted_iota(jnp.int32, (1, H_FOX), 1)

    def query_bias(rows, hh):
        sel = head_lane == pair * 2 + hh
        return jnp.sum(jnp.where(sel, ccol_ref[0, rows, :], 0.0), axis=1, keepdims=True)

    def key_bias_meta(hh):
        return crm_ref[0, pl.ds(pair * 2 + hh, 1), :]

    def fx_init(tq):
        return (jnp.full((tq, 1), NEG_BIG, F32), jnp.zeros((tq, 1), F32), jnp.zeros((tq, HEAD_DIM), F32))

    for hh in range(2):
        lanes = slice(hh * HEAD_DIM, (hh + 1) * HEAD_DIM)
        o_meta, _ = _sb_tile(qs[hh, meta, :], ks[hh, meta, :], vs[hh, meta, :], jnp.zeros((N_META, 1), F32), True)
        osb_ref[0, meta, lanes] = o_meta
        g = 2 + hh
        _, l, a = _fx_tile(qs[g, meta, :], ks[g, meta, :], vs[g, meta, :], query_bias(meta, hh), key_bias_meta(hh),
                           *fx_init(N_META), True)
        ofx_ref[0, meta, lanes] = a / l

    def tiles(rows, keys, ck_of, diag, first, off=0):
        tq = Q_TILE
        tk = keys.size
        k = [ks[g, keys, :] for g in range(4)]
        z = [_dot_nt(qs[g, rows, :], k[g]) for g in range(4)]
        if diag:
            row = lax.broadcasted_iota(jnp.int32, (tq, tk), 0)
            col_id = lax.broadcasted_iota(jnp.int32, (tq, tk), 1)
            strict = col_id < row + off
            incl = col_id <= row + off
        ls = [_log_sigmoid_fast(z[hh]) for hh in range(2)]
        l1m = [ls[hh] - z[hh] for hh in range(2)]
        if diag:
            l1m = [jnp.where(strict, x, 0.0) for x in l1m]
        parts = [p for hh in range(2) for p in _pieces(l1m[hh], 2)]
        u = u_scr[...] if tk == ATTN_BLOCK else _tri(tk, "suffix")
        sums = jnp.dot(jnp.concatenate(parts, axis=0), u, preferred_element_type=F32)
        suffix = [sums[(2 * hh) * tq:(2 * hh + 1) * tq] + sums[(2 * hh + 1) * tq:(2 * hh + 2) * tq]
                  for hh in range(2)]
        fx = []
        for hh in range(2):
            g = 2 + hh
            m_old, l_old, a_old = fx_init(tq) if first else (col[2 + hh], col[4 + hh], acc[g])
            zf = z[g] + (col[6 + hh] - ck_of(hh))
            if diag:
                zf = jnp.where(incl, zf, -jnp.inf)
            m_new = jnp.maximum(m_old, jnp.max(zf, axis=-1, keepdims=True))
            alpha = jnp.exp(m_old - m_new)
            p = jnp.exp(zf - m_new)
            col[2 + hh] = m_new
            col[4 + hh] = alpha * l_old + jnp.sum(p, axis=-1, keepdims=True)
            fx.append((alpha * a_old, p.astype(BF16)))
        for hh in range(2):
            g = 2 + hh
            acc[g] = fx[hh][0] + jnp.dot(fx[hh][1], vs[g, keys, :], preferred_element_type=F32)
        for hh in range(2):
            tail = jnp.zeros((tq, 1), F32) if first else col[hh]
            w = jnp.exp(ls[hh] + suffix[hh] + tail)
            if diag:
                w = jnp.where(strict, w, 0.0)
            o = jnp.dot(w.astype(BF16), vs[hh, keys, :], preferred_element_type=F32)
            acc[hh] = o if first else acc[hh] + o
            col[hh] = tail + suffix[hh][:, 0:1] + l1m[hh][:, 0:1]

    def q_block(qi, carry):
        r0 = pl.multiple_of(N_META + qi * Q_TILE, 16)
        rows = pl.ds(r0, Q_TILE)
        for hh in range(2):
            col[6 + hh] = query_bias(rows, hh)

        def key_tile(kj):
            return pl.ds(pl.multiple_of(N_META + kj * ATTN_BLOCK, 16), ATTN_BLOCK)

        def key_bias(kj):
            c0 = pl.multiple_of(kj * ATTN_BLOCK, ATTN_BLOCK)
            return lambda hh: crr_ref[0, pl.ds(pair * 2 + hh, 1), pl.ds(c0, ATTN_BLOCK)]

        kd = (qi * Q_TILE) // ATTN_BLOCK
        tiles(rows, key_tile(kd), key_bias(kd), True, True, qi * Q_TILE - kd * ATTN_BLOCK)

        def k_block(step, c):
            kj = kd - 1 - step
            tiles(rows, key_tile(kj), key_bias(kj), False, False)
            return c

        lax.fori_loop(0, kd, k_block, 0)
        tiles(rows, meta, key_bias_meta, False, False)
        for hh in range(2):
            lanes = slice(hh * HEAD_DIM, (hh + 1) * HEAD_DIM)
            osb_ref[0, rows, lanes] = acc[hh]
            ofx_ref[0, rows, lanes] = acc[2 + hh] / col[4 + hh]
        return carry

    lax.fori_loop(0, n_blocks, q_block, 0)


def _prompt_attn(q, k, v, ccol, crow_meta, crow):
    b, t, _ = q.shape
    n_pairs = H_SB // 2
    sb_col = pl.BlockSpec((1, t, LANES), lambda i, j: (i, 0, j))
    fx_col = pl.BlockSpec((1, t, LANES), lambda i, j: (i, 0, j + n_pairs))
    whole = lambda a: pl.BlockSpec((1,) + a.shape[1:], lambda i, j: (i, 0, 0))
    return pl.pallas_call(
        _prompt_attn_body,
        grid=(b, n_pairs),
        in_specs=[sb_col, sb_col, sb_col, fx_col, fx_col, fx_col, whole(ccol), whole(crow_meta), whole(crow)],
        out_specs=[sb_col, sb_col],
        out_shape=[jax.ShapeDtypeStruct((b, t, W_SB), F32), jax.ShapeDtypeStruct((b, t, W_FOX), F32)],
        scratch_shapes=[pltpu.VMEM((4, t, HEAD_DIM), BF16)] * 3
                       + [pltpu.VMEM((4, Q_TILE, HEAD_DIM), F32), pltpu.VMEM((8, Q_TILE, 1), F32),
                          pltpu.VMEM((ATTN_BLOCK, ATTN_BLOCK), BF16)],
        compiler_params=_cparams("parallel", "parallel"),
        name="prompt_attn",
    )(q, k, v, q, k, v, ccol, crow_meta, crow)


def _sample_body(n_new, pt_ref, q_ref, kn_ref, vn_ref, lfn_ref, *refs):
    npg = PAGES_PER_STEP
    k_refs = refs[0:npg]
    v_refs = refs[npg:2 * npg]
    lf_refs = refs[2 * npg:3 * npg]
    o_ref = refs[3 * npg]
    q_scr, acc, m_scr, l_scr, tail_scr, rc_scr, cq_scr, u_scr = refs[3 * npg + 1:]
    sb_rows = H_SB * n_new
    step = pl.program_id(1)
    tk = npg * PAGE_SIZE
    head_rows = lambda x, h: x[h * n_new:(h + 1) * n_new]
    head_lanes = lambda h: slice(h * HEAD_DIM, (h + 1) * HEAD_DIM)

    @pl.when(step == 0)
    def _():
        for h in range(N_HEADS):
            q_scr[h] = q_ref[:, head_lanes(h)].astype(F32)
        s_new = jnp.concatenate([_dot_nt(q_scr[h], kn_ref[:, head_lanes(h)]) for h in range(N_HEADS)], axis=0)
        qi = lax.broadcasted_iota(jnp.int32, (N_HEADS * n_new, n_new), 0) % n_new
        ki = lax.broadcasted_iota(jnp.int32, (N_HEADS * n_new, n_new), 1)
        z = s_new[:sb_rows]
        strict = (ki < qi)[:sb_rows]
        ls = _log_sigmoid_fast(z)
        l1m = jnp.where(strict, ls - z, 0.0)
        suffix = _split_dot(l1m, _tri(n_new, "suffix"), 3)
        w_sb = jnp.where(strict, jnp.exp(ls + suffix), 0.0)
        tail_scr[...] = jnp.sum(l1m, axis=-1, keepdims=True)
        cn = _split_dot_rhs(_tri(n_new, "prefix"), lfn_ref[...], 3)
        cn_t = _transpose_small(cn)
        cq = jnp.concatenate([cn[:, h:h + 1] for h in range(H_FOX)], axis=0)
        ck = jnp.concatenate([jnp.broadcast_to(cn_t[h:h + 1, :], (n_new, n_new)) for h in range(H_FOX)], axis=0)
        cq_scr[...] = cq
        zf = jnp.where((ki <= qi)[sb_rows:], s_new[sb_rows:] + (cq - ck), -jnp.inf)
        m0 = jnp.max(zf, axis=-1, keepdims=True)
        p = jnp.exp(zf - m0)
        m_scr[...] = m0
        l_scr[...] = jnp.sum(p, axis=-1, keepdims=True)
        rc_scr[...] = jnp.zeros_like(rc_scr)
        u_scr[...] = _tri(tk, "suffix")
        pw = jnp.concatenate([w_sb, p], axis=0)
        acc[...] = jnp.concatenate(
            [jnp.dot(head_rows(pw, h), vn_ref[:, head_lanes(h)], preferred_element_type=F32)
             for h in range(N_HEADS)], axis=0)

    def head_t(refs_, h):
        return jnp.concatenate([r[h].astype(BF16) for r in refs_], axis=1)

    s = jnp.concatenate([jnp.dot(q_scr[h].astype(BF16), head_t(k_refs, h), preferred_element_type=F32)
                         for h in range(N_HEADS)], axis=0)
    u = u_scr[...]
    z = s[:sb_rows]
    ls = _log_sigmoid_fast(z)
    l1m = ls - z
    suffix = _split_dot(l1m, u, 2)
    tail = tail_scr[...]
    w_sb = jnp.exp(ls + suffix + tail)
    tail_scr[...] = tail + suffix[:, 0:1] + l1m[:, 0:1]
    plf_t = jnp.concatenate([lf_refs[j][...] for j in range(npg)], axis=1)
    rc_prev = rc_scr[...]
    rc = _split_dot(plf_t, u, 3) + rc_prev
    rc_scr[...] = rc_prev + jnp.sum(plf_t, axis=-1, keepdims=True)
    bias = jnp.concatenate([jnp.broadcast_to(rc[h:h + 1, :], (n_new, tk)) for h in range(H_FOX)], axis=0)
    zf = s[sb_rows:] + (cq_scr[...] + bias)
    m_prev = m_scr[...]
    m_new = jnp.maximum(m_prev, jnp.max(zf, axis=-1, keepdims=True))
    alpha = jnp.exp(m_prev - m_new)
    p = jnp.exp(zf - m_new)
    m_scr[...] = m_new
    l_scr[...] = alpha * l_scr[...] + jnp.sum(p, axis=-1, keepdims=True)
    pw = jnp.concatenate([w_sb, p], axis=0)
    scale = jnp.concatenate([jnp.ones((sb_rows, 1), F32), alpha], axis=0)
    pv = jnp.concatenate(
        [_dot_nt(head_rows(pw, h).astype(BF16), head_t(v_refs, h)) for h in range(N_HEADS)], axis=0)
    acc[...] = acc[...] * scale + pv

    @pl.when(step == pl.num_programs(1) - 1)
    def _():
        norm = jnp.concatenate([jnp.ones((sb_rows, 1), F32), 1.0 / l_scr[...]], axis=0)
        out = acc[...] * norm
        for h in range(N_HEADS):
            o_ref[:, head_lanes(h)] = head_rows(out, h)


def _sample_attn(layer, page_table, q, k_new, v_new, logf_new, cache_k, cache_v, cache_logf, n_new):
    db, n_pages = page_table.shape
    npg = PAGES_PER_STEP
    n_steps = n_pages // npg
    rows = N_HEADS * n_new
    fx_rows = H_FOX * n_new

    def page_of(b, s, pt, j):
        return pt[b, n_pages - (s + 1) * npg + j]

    per_row = lambda c: pl.BlockSpec((n_new, c), lambda b, s, pt: (b, 0))
    kv_specs = [pl.BlockSpec((None, None, N_HEADS, HEAD_DIM, PAGE_SIZE),
                             functools.partial(lambda b, s, pt, j: (layer, page_of(b, s, pt, j), 0, 0, 0), j=j))
                for j in range(npg)]
    lf_specs = [pl.BlockSpec((None, None, H_FOX, PAGE_SIZE),
                             functools.partial(lambda b, s, pt, j: (layer, page_of(b, s, pt, j), 0, 0), j=j))
                for j in range(npg)]
    grid_spec = pltpu.PrefetchScalarGridSpec(
        num_scalar_prefetch=1,
        grid=(db, n_steps),
        in_specs=[per_row(D_MODEL), per_row(D_MODEL), per_row(D_MODEL), per_row(H_FOX)]
                 + kv_specs + kv_specs + lf_specs,
        out_specs=per_row(D_MODEL),
        scratch_shapes=[pltpu.VMEM((N_HEADS, n_new, HEAD_DIM), F32),
                        pltpu.VMEM((rows, HEAD_DIM), F32),
                        pltpu.VMEM((fx_rows, 1), F32),
                        pltpu.VMEM((fx_rows, 1), F32),
                        pltpu.VMEM((rows - fx_rows, 1), F32),
                        pltpu.VMEM((H_FOX, 1), F32),
                        pltpu.VMEM((fx_rows, 1), F32),
                        pltpu.VMEM((npg * PAGE_SIZE, npg * PAGE_SIZE), BF16)],
    )
    return pl.pallas_call(
        functools.partial(_sample_body, n_new),
        grid_spec=grid_spec,
        out_shape=jax.ShapeDtypeStruct((db * n_new, D_MODEL), F32),
        compiler_params=_cparams("parallel", "arbitrary"),
        name="sample_attn",
    )(page_table, q, k_new, v_new, logf_new, *([cache_k] * npg), *([cache_v] * npg), *([cache_logf] * npg))


def _route(logits):
    rows = logits.shape[0]
    g = logits[:, :N_GROUPS]
    g_lane = lax.broadcasted_iota(jnp.int32, (rows, N_GROUPS), 1).astype(F32)
    g_max = jnp.max(g, axis=1, keepdims=True)
    grp = jnp.min(jnp.where(g == g_max, g_lane, float(N_GROUPS)), axis=1, keepdims=True)
    p_grp = 1.0 / jnp.sum(jnp.exp(g - g_max), axis=1, keepdims=True)
    e = logits[:, N_GROUPS:]
    lane = lax.broadcasted_iota(jnp.int32, (rows, N_EXPERTS), 1)
    e_lane = lane.astype(F32)
    in_group = (lane // EXPERTS_PER_GROUP).astype(F32) == grp
    m1 = jnp.where(in_group, e, -jnp.inf)
    v1 = jnp.max(m1, axis=1, keepdims=True)
    i1 = jnp.min(jnp.where(m1 == v1, e_lane, float(N_EXPERTS)), axis=1, keepdims=True)
    m2 = jnp.where(e_lane == i1, -jnp.inf, m1)
    v2 = jnp.max(m2, axis=1, keepdims=True)
    i2 = jnp.min(jnp.where(m2 == v2, e_lane, float(N_EXPERTS)), axis=1, keepdims=True)
    t = jnp.exp(v2 - v1)
    g1 = p_grp / (1.0 + t)
    return i1, i2, g1, g1 * t


def _merge_body(x_ref, o_ref, gate_ref, gsb_ref, gfx_ref, wo_ref, gffn_ref, wr_ref, br_ref,
                xmid_ref, h_ref, eid_ref, egate_ref):
    a_sb = (_rms_scale(o_ref[:, :W_SB]) * gsb_ref[...]).astype(BF16)
    a_fx = (_rms_scale(o_ref[:, W_SB:]) * gfx_ref[...] * jax.nn.sigmoid(gate_ref[...].astype(F32))).astype(BF16)
    y = (jnp.dot(a_sb, wo_ref[:W_SB, :], preferred_element_type=F32)
         + jnp.dot(a_fx, wo_ref[W_SB:, :], preferred_element_type=F32))
    x = x_ref[...] + y
    xmid_ref[...] = x
    h = _rms_scale(x) * gffn_ref[...]
    _store_row_tiled(h_ref, h)
    logits = jnp.dot(h, wr_ref[...], preferred_element_type=F32, precision=lax.Precision.HIGHEST) + br_ref[...]
    i1, i2, g1, g2 = _route(logits)
    eid_ref[:, 0:1] = i1.astype(jnp.int32)
    eid_ref[:, 1:2] = i2.astype(jnp.int32)
    egate_ref[:, 0:1] = g1
    egate_ref[:, 1:2] = g2


def _merge(x, o_parts, gate, g_sb, g_fx, w_out, g_ffn, w_route, b_route, tm):
    n = x.shape[0]
    row = lambda c: pl.BlockSpec((tm, c), lambda i: (i, 0))
    full = lambda a: pl.BlockSpec(a.shape, lambda i: (0,) * a.ndim)
    body = _merge_body
    if len(o_parts) == 2:
        def body(x_ref, osb_ref, ofx_ref, *rest):
            return _merge_body(x_ref, _Halves(osb_ref, ofx_ref), *rest)
    return pl.pallas_call(
        body,
        grid=(n // tm,),
        in_specs=[row(D_MODEL)] + [row(o.shape[1]) for o in o_parts]
                 + [row(W_FOX), full(g_sb), full(g_fx), full(w_out), full(g_ffn), full(w_route), full(b_route)],
        out_specs=[row(D_MODEL), pl.BlockSpec((tm * ROW_TILES, LANES), lambda i: (i, 0)), row(2), row(2)],
        out_shape=[jax.ShapeDtypeStruct((n, D_MODEL), F32),
                   jax.ShapeDtypeStruct((n * ROW_TILES, LANES), F32),
                   jax.ShapeDtypeStruct((n, 2), jnp.int32),
                   jax.ShapeDtypeStruct((n, 2), F32)],
        compiler_params=_cparams("parallel"),
        name="merge",
    )(x, *o_parts, gate, g_sb, g_fx, w_out, g_ffn, w_route, b_route)


class _Halves:
    def __init__(self, lo, hi):
        self.lo, self.hi = lo, hi

    def __getitem__(self, idx):
        rows, cols = idx
        if cols == slice(None, W_SB):
            return self.lo[rows, :]
        assert cols == slice(W_SB, None)
        return self.hi[rows, :]


def _expert_body(n_tok, be_ref, nv_ref, dst_ref, h_hbm, gate_ref, w1_ref, w3_ref, w2_ref, y_hbm,
                 xbuf, ybuf, gsem, ssem):
    i = pl.program_id(0)
    nb = pl.num_programs(0)
    slot = i % 2
    tile = lambda r: pl.ds(pl.multiple_of(r * ROW_TILES, ROW_TILES), ROW_TILES)

    def gather(b, sl):
        def one(r, c):
            d = dst_ref[b * MOE_BLOCK + r]
            tok = jnp.where(d >= n_tok, d - n_tok, d)
            pltpu.make_async_copy(h_hbm.at[tile(tok), :], xbuf.at[sl, tile(r), :], gsem.at[sl]).start()
            return c
        lax.fori_loop(0, nv_ref[b], one, 0)

    def gather_wait(b, sl):
        n = nv_ref[b] * ROW_TILES

        @pl.when(n > 0)
        def _():
            pltpu.make_async_copy(h_hbm.at[pl.ds(0, n), :], xbuf.at[sl, pl.ds(0, n), :], gsem.at[sl]).wait()

    def scatter(b, sl):
        def one(r, c):
            d = dst_ref[b * MOE_BLOCK + r]
            pltpu.make_async_copy(ybuf.at[sl, tile(r), :], y_hbm.at[tile(d), :], ssem.at[sl]).start()
            return c
        lax.fori_loop(0, nv_ref[b], one, 0)

    def scatter_wait(b, sl):
        n = nv_ref[b] * ROW_TILES

        @pl.when(n > 0)
        def _():
            pltpu.make_async_copy(ybuf.at[sl, pl.ds(0, n), :], y_hbm.at[pl.ds(0, n), :], ssem.at[sl]).wait()

    @pl.when(i == 0)
    def _():
        xbuf[...] = jnp.zeros_like(xbuf)
        gather(0, 0)

    @pl.when(i + 1 < nb)
    def _():
        gather(i + 1, 1 - slot)

    gather_wait(i, slot)

    @pl.when(i >= 2)
    def _():
        scatter_wait(i - 2, slot)

    @pl.when(nv_ref[i] > 0)
    def _():
        x = _load_row_tiled(xbuf.at[slot], MOE_BLOCK).astype(BF16)
        a = jnp.dot(x, w1_ref[...].astype(BF16), preferred_element_type=F32)
        b = jnp.dot(x, w3_ref[...].astype(BF16), preferred_element_type=F32)
        hmid = (a * jax.nn.sigmoid(a) * b).astype(BF16)
        y = jnp.dot(hmid, w2_ref[...].astype(BF16), preferred_element_type=F32) * gate_ref[...]
        _store_row_tiled(ybuf, y, (slot,))
        scatter(i, slot)

    @pl.when(i == nb - 1)
    def _():
        @pl.when(i >= 1)
        def _():
            scatter_wait(i - 1, 1 - slot)
        scatter_wait(i, slot)


def _experts(n_tok, block_exp, n_valid, dst_row, h_rt, slot_gate, layer, w1, w3, w2):
    n_blocks = block_exp.shape[0]
    buf = pltpu.VMEM((2, MOE_BLOCK * ROW_TILES, LANES), F32)
    expert = lambda i, be, nv, dst: (layer, be[i], 0, 0)
    grid_spec = pltpu.PrefetchScalarGridSpec(
        num_scalar_prefetch=3,
        grid=(n_blocks,),
        in_specs=[pl.BlockSpec(memory_space=pl.ANY),
                  pl.BlockSpec((MOE_BLOCK, 1), lambda i, be, nv, dst: (i, 0)),
                  pl.BlockSpec((None, None, D_MODEL, D_EXPERT), expert),
                  pl.BlockSpec((None, None, D_MODEL, D_EXPERT), expert),
                  pl.BlockSpec((None, None, D_EXPERT, D_MODEL), expert)],
        out_specs=pl.BlockSpec(memory_space=pl.ANY),
        scratch_shapes=[buf, buf, pltpu.SemaphoreType.DMA((2,)), pltpu.SemaphoreType.DMA((2,))],
    )
    return pl.pallas_call(
        functools.partial(_expert_body, n_tok),
        grid_spec=grid_spec,
        out_shape=jax.ShapeDtypeStruct((2 * n_tok * ROW_TILES, LANES), F32),
        compiler_params=_cparams("arbitrary"),
        name="experts",
    )(block_exp, n_valid, dst_row, h_rt, slot_gate, w1, w3, w2)


def _moe(h_rt, eid, gate, layer, w1, w3, w2):
    n_tok = eid.shape[0]
    n_asg = n_tok * 2
    flat_e = eid.reshape(n_asg)
    order = jnp.argsort(flat_e).astype(jnp.int32)
    counts = jnp.sum((flat_e[:, None] == jnp.arange(N_EXPERTS, dtype=jnp.int32)[None, :]).astype(jnp.int32), axis=0)
    padded = (counts + MOE_BLOCK - 1) // MOE_BLOCK * MOE_BLOCK
    ends_p = jnp.cumsum(padded)
    run_start = ends_p - padded
    sorted_start = jnp.cumsum(counts) - counts
    n_blocks = -(-n_asg // MOE_BLOCK) + N_EXPERTS
    block_start = jnp.arange(n_blocks, dtype=jnp.int32) * MOE_BLOCK
    block_exp = jnp.minimum(jnp.sum((ends_p[None, :] <= block_start[:, None]).astype(jnp.int32), axis=1),
                            N_EXPERTS - 1).astype(jnp.int32)
    block_off = block_start - run_start[block_exp]
    n_valid = jnp.clip(counts[block_exp] - block_off, 0, MOE_BLOCK).astype(jnp.int32)
    in_block = jnp.arange(MOE_BLOCK, dtype=jnp.int32)[None, :]
    valid = in_block < n_valid[:, None]
    pos = jnp.clip((sorted_start[block_exp] + block_off)[:, None] + in_block, 0, n_asg - 1)
    asg = order[pos.reshape(-1)]
    dst_row = jnp.where(valid.reshape(-1), (asg % 2) * n_tok + asg // 2, 0).astype(jnp.int32)
    slot_gate = jnp.where(valid.reshape(-1), gate.reshape(n_asg)[asg], 0.0)
    return _experts(n_tok, block_exp, n_valid, dst_row, h_rt, slot_gate[:, None], layer, w1, w3, w2)


def _final_body(x_ref, ya_ref, yb_ref, g_ref, o_ref):
    rows = x_ref.shape[0]
    x = x_ref[...] + (_load_row_tiled(ya_ref, rows) + _load_row_tiled(yb_ref, rows))
    o_ref[...] = _rms_scale(x) * g_ref[...]


def _final_norm(x, y2, g, tm):
    n = x.shape[0]
    nt = n // tm
    row = pl.BlockSpec((tm, D_MODEL), lambda i: (i, 0))
    return pl.pallas_call(
        _final_body,
        grid=(nt,),
        in_specs=[row,
                  pl.BlockSpec((tm * ROW_TILES, LANES), lambda i: (i, 0)),
                  pl.BlockSpec((tm * ROW_TILES, LANES), lambda i: (i + nt, 0)),
                  pl.BlockSpec(g.shape, lambda i: (0, 0))],
        out_specs=row,
        out_shape=jax.ShapeDtypeStruct((n, D_MODEL), F32),
        compiler_params=_cparams("parallel"),
        name="final_norm",
    )(x, y2, y2, g)


def _row_tile(n):
    for tm in (768, 512, 384, 256, 128, 64, 32, 16, 8):
        if n % tm == 0:
            return tm
    raise ValueError(f"row count {n} is not a multiple of 8")


def kernel(x_prompt, x_sample, cache_k, cache_v, cache_logf, page_table, meta_tokens, norm_mix_g, w_in, b_forget,
           norm_sb_g, norm_fox_g, w_out, norm_ffn_g, w_group, b_group, w_router, b_router, w_exp_gate, w_exp_up,
           w_exp_down, norm_final_g):
    bsz, seq, _ = x_prompt.shape
    db, n_new, _ = x_sample.shape
    depth = w_in.shape[0]
    t = seq + N_META
    n_p = bsz * t
    n_s = db * n_new
    assert (seq % ATTN_BLOCK) == 0 and page_table.shape[1] % PAGES_PER_STEP == 0

    meta = jnp.broadcast_to(meta_tokens[None].astype(x_prompt.dtype), (bsz, N_META, D_MODEL))
    xp = jnp.concatenate([meta, x_prompt], axis=1).reshape(n_p, D_MODEL)
    xs = x_sample.reshape(n_s, D_MODEL)
    ck_t = jnp.transpose(cache_k, (0, 1, 3, 4, 2))
    cv_t = jnp.transpose(cache_v, (0, 1, 3, 4, 2))
    clf_t = jnp.transpose(cache_logf, (0, 1, 3, 2))
    tm_p, tm_s = _row_tile(n_p), _row_tile(n_s)
    n_main = 3 * D_MODEL + W_FOX

    streams = {"p": (xp, None), "s": (xs, None)}
    outs = {name: {"k": [], "v": [], "lf": []} for name in streams}
    for l in range(depth):
        w_main = w_in[l, :, :n_main].astype(BF16)
        w_f = w_in[l, :, n_main:].astype(BF16)
        b_f = b_forget[l][None, :]
        g_mix = norm_mix_g[l][None, :]
        w_o = w_out[l].astype(BF16)
        w_route = jnp.concatenate([w_group[l], w_router[l]], axis=1)
        b_route = jnp.concatenate([b_group[l], b_router[l]])[None, :]
        new_streams = {}
        for name, tm in (("p", tm_p), ("s", tm_s)):
            x, y2 = streams[name]
            r = _inproj(x, y2, g_mix, w_main, w_f, b_f, tm)
            if y2 is not None:
                x, r = r[0], r[1:]
            q, k, v, gate, logf = r
            outs[name]["k"].append(k)
            outs[name]["v"].append(v)
            outs[name]["lf"].append(logf)
            if name == "p":
                q3, k3, v3 = (a.reshape(bsz, t, D_MODEL) for a in (q, k, v))
                ccol, crow_meta, crow = _forget_cumsum(logf.reshape(bsz, t, H_FOX))
                o_sb, o_fx = _prompt_attn(q3, k3, v3, ccol, crow_meta, crow)
                o_parts = (o_sb.reshape(n_p, W_SB), o_fx.reshape(n_p, W_FOX))
            else:
                o_parts = (_sample_attn(l, page_table, q, k, v, logf, ck_t, cv_t, clf_t, n_new),)
            x_mid, h_rt, eid, egate = _merge(x, o_parts, gate, norm_sb_g[l][None, :], norm_fox_g[l][None, :], w_o,
                                             norm_ffn_g[l][None, :], w_route, b_route, tm)
            new_streams[name] = (x_mid, _moe(h_rt, eid, egate, l, w_exp_gate, w_exp_up, w_exp_down))
        streams = new_streams

    g_fin = norm_final_g[None, :]
    y_prompt = _final_norm(*streams["p"], g_fin, tm_p).reshape(bsz, t, D_MODEL)[:, N_META:]
    y_sample = _final_norm(*streams["s"], g_fin, tm_s).reshape(db, n_new, D_MODEL)
    heads_p = (bsz, t, N_HEADS, HEAD_DIM)
    heads_s = (db, n_new, N_HEADS, HEAD_DIM)
    return (y_prompt, y_sample,
            jnp.stack([a.reshape(heads_p) for a in outs["p"]["k"]]),
            jnp.stack([a.reshape(heads_p) for a in outs["p"]["v"]]),
            jnp.stack([a.reshape(bsz, t, H_FOX) for a in outs["p"]["lf"]]),
            jnp.stack([a.reshape(heads_s) for a in outs["s"]["k"]]),
            jnp.stack([a.reshape(heads_s) for a in outs["s"]["v"]]),
            jnp.stack([a.reshape(db, n_new, H_FOX) for a in outs["s"]["lf"]]))
```

```python
import functools

import jax
import jax.numpy as jnp
from jax import lax
from jax.experimental import pallas as pl
from jax.experimental.pallas import tpu as pltpu

F32 = jnp.float32
BF16 = jnp.bfloat16

D_MODEL = 1024
HEAD_DIM = 64
H_SB = 8
H_FOX = 8
N_HEADS = H_SB + H_FOX
W_SB = H_SB * HEAD_DIM
W_FOX = H_FOX * HEAD_DIM
N_META = 16
N_GROUPS = 4
EXPERTS_PER_GROUP = 8
N_EXPERTS = N_GROUPS * EXPERTS_PER_GROUP
D_EXPERT = D_MODEL // 2
MOE_BLOCK = 256
RMS_EPS = 1e-6
PAGE_SIZE = 128
LANES = 128
SUBLANES = 8
ROW_TILES = D_MODEL // LANES
ATTN_BLOCK = 256
Q_TILE = 128
PAGES_PER_STEP = 8
NEG_BIG = -1e30
VMEM_LIMIT = 56 * 1024 * 1024


def _cparams(*sem):
    return pltpu.CompilerParams(dimension_semantics=sem, vmem_limit_bytes=VMEM_LIMIT)


def _log_sigmoid_fast(z):
    return jnp.minimum(z, 0.0) - jnp.log(1.0 + jnp.exp(-jnp.abs(z)))


def _log_sigmoid(x):
    return jnp.minimum(x, 0.0) - jnp.log1p(jnp.exp(-jnp.abs(x)))


def _rms_scale(x):
    return x * lax.rsqrt(jnp.mean(x * x, axis=-1, keepdims=True) + RMS_EPS)


def _dot_nt(a, b):
    return lax.dot_general(a, b, (((1,), (1,)), ((), ())), preferred_element_type=F32)


def _tri(n, kind):
    a = lax.broadcasted_iota(jnp.int32, (n, n), 0)
    b = lax.broadcasted_iota(jnp.int32, (n, n), 1)
    m = {"suffix": a > b, "prefix": b <= a, "eye": a == b}[kind]
    return jnp.where(m, 1.0, 0.0).astype(BF16)


def _pieces(x, parts):
    out = []
    rem = x
    for p in range(parts):
        piece = rem.astype(BF16)
        out.append(piece)
        if p + 1 < parts:
            rem = rem - piece.astype(F32)
    return out


def _split_dot(x, m01, parts):
    return sum(jnp.dot(p, m01, preferred_element_type=F32) for p in _pieces(x, parts))


def _split_dot_rhs(m01, x, parts):
    return sum(jnp.dot(m01, p, preferred_element_type=F32) for p in _pieces(x, parts))


def _transpose_small(x, parts=3):
    eye = _tri(x.shape[1], "eye")
    return sum(_dot_nt(eye, p) for p in _pieces(x, parts))


def _load_row_tiled(ref, rows):
    return jnp.concatenate([ref[pl.ds(s, rows, stride=ROW_TILES), :] for s in range(ROW_TILES)], axis=1)


def _store_row_tiled(ref, x, lead=()):
    rows = x.shape[0]
    for s in range(ROW_TILES):
        ref[lead + (pl.ds(s, rows, stride=ROW_TILES), slice(None))] = x[:, s * LANES:(s + 1) * LANES]


def _inproj_body(has_res, *refs):
    if has_res:
        x_ref, ya_ref, yb_ref, g_ref, w_ref, wf_ref, bf_ref, xo_ref, q_ref, k_ref, v_ref, gate_ref, logf_ref = refs
        rows = x_ref.shape[0]
        x = x_ref[...] + (_load_row_tiled(ya_ref, rows) + _load_row_tiled(yb_ref, rows))
        xo_ref[...] = x
    else:
        x_ref, g_ref, w_ref, wf_ref, bf_ref, q_ref, k_ref, v_ref, gate_ref, logf_ref = refs
        x = x_ref[...]
    h = (_rms_scale(x) * g_ref[...]).astype(BF16)
    half = D_MODEL // 2
    q_scale = HEAD_DIM ** -0.5
    for c in range(7):
        z = jnp.dot(h, w_ref[:, c * half:(c + 1) * half], preferred_element_type=F32)
        dst = pl.ds((c % 2) * half, half)
        if c < 2:
            q_ref[:, dst] = (z * q_scale).astype(BF16)
        elif c < 4:
            k_ref[:, dst] = z
        elif c < 6:
            v_ref[:, dst] = z
        else:
            gate_ref[...] = z.astype(BF16)
    zf = jnp.dot(h, wf_ref[...], preferred_element_type=F32) + bf_ref[...]
    logf_ref[...] = _log_sigmoid(zf)


def _inproj(x, y2, g, w_main, w_f, b_f, tm):
    n = x.shape[0]
    nt = n // tm
    row = lambda c: pl.BlockSpec((tm, c), lambda i: (i, 0))
    full = lambda a: pl.BlockSpec(a.shape, lambda i: (0,) * a.ndim)
    out_shape = [jax.ShapeDtypeStruct((n, D_MODEL), BF16),
                 jax.ShapeDtypeStruct((n, D_MODEL), F32),
                 jax.ShapeDtypeStruct((n, D_MODEL), F32),
                 jax.ShapeDtypeStruct((n, W_FOX), BF16),
                 jax.ShapeDtypeStruct((n, H_FOX), F32)]
    out_specs = [row(D_MODEL), row(D_MODEL), row(D_MODEL), row(W_FOX), row(H_FOX)]
    in_specs = [row(D_MODEL)]
    args = [x]
    if y2 is not None:
        out_shape = [jax.ShapeDtypeStruct((n, D_MODEL), F32)] + out_shape
        out_specs = [row(D_MODEL)] + out_specs
        in_specs += [pl.BlockSpec((tm * ROW_TILES, LANES), lambda i: (i, 0)),
                     pl.BlockSpec((tm * ROW_TILES, LANES), lambda i: (i + nt, 0))]
        args += [y2, y2]
    return pl.pallas_call(
        functools.partial(_inproj_body, y2 is not None),
        grid=(nt,),
        in_specs=in_specs + [full(g), full(w_main), full(w_f), full(b_f)],
        out_specs=out_specs,
        out_shape=out_shape,
        compiler_params=_cparams("parallel"),
        name="inproj",
    )(*args, g, w_main, w_f, b_f)


def _forget_cumsum_body(lf_ref, ccol_ref, crow_meta_ref, crow_ref):
    t = lf_ref.shape[1]
    n_blocks = (t - N_META) // ATTN_BLOCK
    meta = pl.ds(0, N_META)
    c = _split_dot_rhs(_tri(N_META, "prefix"), lf_ref[0, meta, :], 3)
    ccol_ref[0, meta, :] = c
    crow_meta_ref[0] = _transpose_small(c)
    carry0 = c[N_META - 1:N_META, :]
    prefix = _tri(ATTN_BLOCK, "prefix")

    def block(i, carry):
        r0 = pl.multiple_of(N_META + i * ATTN_BLOCK, 16)
        rows = pl.ds(r0, ATTN_BLOCK)
        c = _split_dot_rhs(prefix, lf_ref[0, rows, :], 3) + carry
        ccol_ref[0, rows, :] = c
        crow_ref[0, :, pl.ds(pl.multiple_of(i * ATTN_BLOCK, ATTN_BLOCK), ATTN_BLOCK)] = _transpose_small(c)
        return c[ATTN_BLOCK - 1:ATTN_BLOCK, :]

    lax.fori_loop(0, n_blocks, block, carry0)


def _forget_cumsum(logf):
    b, t, _ = logf.shape
    return pl.pallas_call(
        _forget_cumsum_body,
        grid=(b,),
        in_specs=[pl.BlockSpec((1, t, H_FOX), lambda i: (i, 0, 0))],
        out_specs=[pl.BlockSpec((1, t, H_FOX), lambda i: (i, 0, 0)),
                   pl.BlockSpec((1, H_FOX, N_META), lambda i: (i, 0, 0)),
                   pl.BlockSpec((1, H_FOX, t - N_META), lambda i: (i, 0, 0))],
        out_shape=[jax.ShapeDtypeStruct((b, t, H_FOX), F32),
                   jax.ShapeDtypeStruct((b, H_FOX, N_META), F32),
                   jax.ShapeDtypeStruct((b, H_FOX, t - N_META), F32)],
        compiler_params=_cparams("parallel"),
        name="forget_cumsum",
    )(logf)


def _sb_tile(q, k, v, tail, diag, off=0):
    tq, tk = q.shape[0], k.shape[0]
    z = _dot_nt(q, k)
    ls = _log_sigmoid_fast(z)
    l1m = ls - z
    if diag:
        row = lax.broadcasted_iota(jnp.int32, (tq, tk), 0)
        col = lax.broadcasted_iota(jnp.int32, (tq, tk), 1)
        valid = col < row + off
        l1m = jnp.where(valid, l1m, 0.0)
    suffix = _split_dot(l1m, _tri(tk, "suffix"), 2)
    w = jnp.exp(ls + suffix + tail)
    if diag:
        w = jnp.where(valid, w, 0.0)
    o = jnp.dot(w.astype(BF16), v, preferred_element_type=F32)
    tail = tail + suffix[:, 0:1] + l1m[:, 0:1]
    return o, tail


def _fx_tile(q, k, v, cq, ck, m, l, acc, diag, off=0):
    tq, tk = q.shape[0], k.shape[0]
    z = _dot_nt(q, k) + (cq - ck)
    if diag:
        row = lax.broadcasted_iota(jnp.int32, (tq, tk), 0)
        col = lax.broadcasted_iota(jnp.int32, (tq, tk), 1)
        z = jnp.where(col <= row + off, z, -jnp.inf)
    m_new = jnp.maximum(m, jnp.max(z, axis=-1, keepdims=True))
    alpha = jnp.exp(m - m_new)
    p = jnp.exp(z - m_new)
    l = alpha * l + jnp.sum(p, axis=-1, keepdims=True)
    acc = alpha * acc + jnp.dot(p.astype(BF16), v, preferred_element_type=F32)
    return m_new, l, acc


def _prompt_attn_body(qsb_ref, ksb_ref, vsb_ref, qfx_ref, kfx_ref, vfx_ref, ccol_ref, crm_ref, crr_ref,
                      osb_ref, ofx_ref, qs, ks, vs, acc, col, u_scr, z_scr, zm_scr):
    u_scr[...] = _tri(ATTN_BLOCK, "suffix")
    for g, (qr, kr, vr) in enumerate(((qsb_ref, ksb_ref, vsb_ref), (qfx_ref, kfx_ref, vfx_ref))):
        for hh in range(2):
            lanes = slice(hh * HEAD_DIM, (hh + 1) * HEAD_DIM)
            qs[2 * g + hh] = qr[0, :, lanes]
            ks[2 * g + hh] = kr[0, :, lanes].astype(BF16)
            vs[2 * g + hh] = vr[0, :, lanes].astype(BF16)
    t = qsb_ref.shape[1]
    n_blocks = (t - N_META) // Q_TILE
    pair = pl.program_id(1)
    meta = pl.ds(0, N_META)
    head_lane = lax.broadcasted_iota(jnp.int32, (1, H_FOX), 1)

    def query_bias(rows, hh):
        sel = head_lane == pair * 2 + hh
        return jnp.sum(jnp.where(sel, ccol_ref[0, rows, :], 0.0), axis=1, keepdims=True)

    def key_bias_meta(hh):
        return crm_ref[0, pl.ds(pair * 2 + hh, 1), :]

    def fx_init(tq):
        return (jnp.full((tq, 1), NEG_BIG, F32), jnp.zeros((tq, 1), F32), jnp.zeros((tq, HEAD_DIM), F32))

    for hh in range(2):
        lanes = slice(hh * HEAD_DIM, (hh + 1) * HEAD_DIM)
        o_meta, _ = _sb_tile(qs[hh, meta, :], ks[hh, meta, :], vs[hh, meta, :], jnp.zeros((N_META, 1), F32), True)
        osb_ref[0, meta, lanes] = o_meta
        g = 2 + hh
        _, l, a = _fx_tile(qs[g, meta, :], ks[g, meta, :], vs[g, meta, :], query_bias(meta, hh), key_bias_meta(hh),
                           *fx_init(N_META), True)
        ofx_ref[0, meta, lanes] = a / l

    def scores(rows, keys, z_ref):
        for g in range(4):
            z_ref[g] = _dot_nt(qs[g, rows, :], ks[g, keys, :])

    def q_rows(qi):
        return pl.ds(pl.multiple_of(N_META + qi * Q_TILE, 16), Q_TILE)

    def key_tile(kj):
        return pl.ds(pl.multiple_of(N_META + kj * ATTN_BLOCK, 16), ATTN_BLOCK)

    def diag_tile(qi):
        return (qi * Q_TILE) // ATTN_BLOCK

    def tiles(rows, keys, ck_of, diag, first, z_ref, prefetch, off=0):
        tq = Q_TILE
        tk = keys.size
        z = [z_ref[g] for g in range(4)]
        prefetch()
        if diag:
            row = lax.broadcasted_iota(jnp.int32, (tq, tk), 0)
            col_id = lax.broadcasted_iota(jnp.int32, (tq, tk), 1)
            strict = col_id < row + off
            incl = col_id <= row + off
        ls = [_log_sigmoid_fast(z[hh]) for hh in range(2)]
        l1m = [ls[hh] - z[hh] for hh in range(2)]
        if diag:
            l1m = [jnp.where(strict, x, 0.0) for x in l1m]
        u = u_scr[...] if tk == ATTN_BLOCK else _tri(tk, "suffix")
        sums = jnp.dot(jnp.concatenate([x.astype(BF16) for x in l1m], axis=0), u, preferred_element_type=F32)
        suffix = [sums[hh * tq:(hh + 1) * tq] for hh in range(2)]
        fx = []
        for hh in range(2):
            g = 2 + hh
            m_old, l_old, a_old = fx_init(tq) if first else (col[2 + hh], col[4 + hh], acc[g])
            zf = z[g] + (col[6 + hh] - ck_of(hh))
            if diag:
                zf = jnp.where(incl, zf, -jnp.inf)
            m_new = jnp.maximum(m_old, jnp.max(zf, axis=-1, keepdims=True))
            alpha = jnp.exp(m_old - m_new)
            p = jnp.exp(zf - m_new)
            col[2 + hh] = m_new
            col[4 + hh] = alpha * l_old + jnp.sum(p, axis=-1, keepdims=True)
            fx.append((alpha * a_old, p.astype(BF16)))
        for hh in range(2):
            g = 2 + hh
            acc[g] = fx[hh][0] + jnp.dot(fx[hh][1], vs[g, keys, :], preferred_element_type=F32)
        for hh in range(2):
            tail = jnp.zeros((tq, 1), F32) if first else col[hh]
            w = jnp.exp(ls[hh] + suffix[hh] + tail)
            if diag:
                w = jnp.where(strict, w, 0.0)
            o = jnp.dot(w.astype(BF16), vs[hh, keys, :], preferred_element_type=F32)
            acc[hh] = o if first else acc[hh] + o
            col[hh] = tail + suffix[hh][:, 0:1] + l1m[hh][:, 0:1]

    def block_scores(qi):
        scores(q_rows(qi), key_tile(diag_tile(qi)), z_scr)
        scores(q_rows(qi), meta, zm_scr)

    def q_block(qi, carry):
        rows = q_rows(qi)
        for hh in range(2):
            col[6 + hh] = query_bias(rows, hh)

        def key_bias(kj):
            c0 = pl.multiple_of(kj * ATTN_BLOCK, ATTN_BLOCK)
            return lambda hh: crr_ref[0, pl.ds(pair * 2 + hh, 1), pl.ds(c0, ATTN_BLOCK)]

        def next_scores(kj):
            return lambda: scores(rows, key_tile(jnp.maximum(kj - 1, 0)), z_scr)

        kd = diag_tile(qi)
        tiles(rows, key_tile(kd), key_bias(kd), True, True, z_scr, next_scores(kd),
              qi * Q_TILE - kd * ATTN_BLOCK)

        def k_block(step, c):
            kj = kd - 1 - step
            tiles(rows, key_tile(kj), key_bias(kj), False, False, z_scr, next_scores(kj))
            return c

        lax.fori_loop(0, kd, k_block, 0)
        tiles(rows, meta, key_bias_meta, False, False, zm_scr,
              lambda: block_scores(jnp.minimum(qi + 1, n_blocks - 1)))
        for hh in range(2):
            lanes = slice(hh * HEAD_DIM, (hh + 1) * HEAD_DIM)
            osb_ref[0, rows, lanes] = acc[hh]
            ofx_ref[0, rows, lanes] = acc[2 + hh] / col[4 + hh]
        return carry

    block_scores(0)
    lax.fori_loop(0, n_blocks, q_block, 0)


def _prompt_attn(q, k, v, ccol, crow_meta, crow):
    b, t, _ = q.shape
    n_pairs = H_SB // 2
    sb_col = pl.BlockSpec((1, t, LANES), lambda i, j: (i, 0, j))
    fx_col = pl.BlockSpec((1, t, LANES), lambda i, j: (i, 0, j + n_pairs))
    whole = lambda a: pl.BlockSpec((1,) + a.shape[1:], lambda i, j: (i, 0, 0))
    return pl.pallas_call(
        _prompt_attn_body,
        grid=(b, n_pairs),
        in_specs=[sb_col, sb_col, sb_col, fx_col, fx_col, fx_col, whole(ccol), whole(crow_meta), whole(crow)],
        out_specs=[sb_col, sb_col],
        out_shape=[jax.ShapeDtypeStruct((b, t, W_SB), F32), jax.ShapeDtypeStruct((b, t, W_FOX), F32)],
        scratch_shapes=[pltpu.VMEM((4, t, HEAD_DIM), BF16)] * 3
                       + [pltpu.VMEM((4, Q_TILE, HEAD_DIM), F32), pltpu.VMEM((8, Q_TILE, 1), F32),
                          pltpu.VMEM((ATTN_BLOCK, ATTN_BLOCK), BF16),
                          pltpu.VMEM((4, Q_TILE, ATTN_BLOCK), F32), pltpu.VMEM((4, Q_TILE, N_META), F32)],
        compiler_params=_cparams("parallel", "parallel"),
        name="prompt_attn",
    )(q, k, v, q, k, v, ccol, crow_meta, crow)


def _sample_body(n_new, pt_ref, q_ref, kn_ref, vn_ref, lfn_ref, *refs):
    npg = PAGES_PER_STEP
    k_refs = refs[0:npg]
    v_refs = refs[npg:2 * npg]
    lf_refs = refs[2 * npg:3 * npg]
    o_ref = refs[3 * npg]
    q_scr, acc, m_scr, l_scr, tail_scr, rc_scr, cq_scr, u_scr = refs[3 * npg + 1:]
    sb_rows = H_SB * n_new
    step = pl.program_id(1)
    tk = npg * PAGE_SIZE
    head_rows = lambda x, h: x[h * n_new:(h + 1) * n_new]
    head_lanes = lambda h: slice(h * HEAD_DIM, (h + 1) * HEAD_DIM)

    @pl.when(step == 0)
    def _():
        for h in range(N_HEADS):
            q_scr[h] = q_ref[:, head_lanes(h)].astype(F32)
        s_new = jnp.concatenate([_dot_nt(q_scr[h], kn_ref[:, head_lanes(h)]) for h in range(N_HEADS)], axis=0)
        qi = lax.broadcasted_iota(jnp.int32, (N_HEADS * n_new, n_new), 0) % n_new
        ki = lax.broadcasted_iota(jnp.int32, (N_HEADS * n_new, n_new), 1)
        z = s_new[:sb_rows]
        strict = (ki < qi)[:sb_rows]
        ls = _log_sigmoid_fast(z)
        l1m = jnp.where(strict, ls - z, 0.0)
        suffix = _split_dot(l1m, _tri(n_new, "suffix"), 3)
        w_sb = jnp.where(strict, jnp.exp(ls + suffix), 0.0)
        tail_scr[...] = jnp.sum(l1m, axis=-1, keepdims=True)
        cn = _split_dot_rhs(_tri(n_new, "prefix"), lfn_ref[...], 3)
        cn_t = _transpose_small(cn)
        cq = jnp.concatenate([cn[:, h:h + 1] for h in range(H_FOX)], axis=0)
        ck = jnp.concatenate([jnp.broadcast_to(cn_t[h:h + 1, :], (n_new, n_new)) for h in range(H_FOX)], axis=0)
        cq_scr[...] = cq
        zf = jnp.where((ki <= qi)[sb_rows:], s_new[sb_rows:] + (cq - ck), -jnp.inf)
        m0 = jnp.max(zf, axis=-1, keepdims=True)
        p = jnp.exp(zf - m0)
        m_scr[...] = m0
        l_scr[...] = jnp.sum(p, axis=-1, keepdims=True)
        rc_scr[...] = jnp.zeros_like(rc_scr)
        u_scr[...] = _tri(tk, "suffix")
        pw = jnp.concatenate([w_sb, p], axis=0)
        acc[...] = jnp.concatenate(
            [jnp.dot(head_rows(pw, h), vn_ref[:, head_lanes(h)], preferred_element_type=F32)
             for h in range(N_HEADS)], axis=0)

    def head_t(refs_, h):
        return jnp.concatenate([r[h].astype(BF16) for r in refs_], axis=1)

    s = jnp.concatenate([jnp.dot(q_scr[h].astype(BF16), head_t(k_refs, h), preferred_element_type=F32)
                         for h in range(N_HEADS)], axis=0)
    u = u_scr[...]
    z = s[:sb_rows]
    ls = _log_sigmoid_fast(z)
    l1m = ls - z
    suffix = _split_dot(l1m, u, 2)
    tail = tail_scr[...]
    w_sb = jnp.exp(ls + suffix + tail)
    tail_scr[...] = tail + suffix[:, 0:1] + l1m[:, 0:1]
    plf_t = jnp.concatenate([lf_refs[j][...] for j in range(npg)], axis=1)
    rc_prev = rc_scr[...]
    rc = _split_dot(plf_t, u, 3) + rc_prev
    rc_scr[...] = rc_prev + jnp.sum(plf_t, axis=-1, keepdims=True)
    bias = jnp.concatenate([jnp.broadcast_to(rc[h:h + 1, :], (n_new, tk)) for h in range(H_FOX)], axis=0)
    zf = s[sb_rows:] + (cq_scr[...] + bias)
    m_prev = m_scr[...]
    m_new = jnp.maximum(m_prev, jnp.max(zf, axis=-1, keepdims=True))
    alpha = jnp.exp(m_prev - m_new)
    p = jnp.exp(zf - m_new)
    m_scr[...] = m_new
    l_scr[...] = alpha * l_scr[...] + jnp.sum(p, axis=-1, keepdims=True)
    pw = jnp.concatenate([w_sb, p], axis=0)
    scale = jnp.concatenate([jnp.ones((sb_rows, 1), F32), alpha], axis=0)
    pv = jnp.concatenate(
        [_dot_nt(head_rows(pw, h).astype(BF16), head_t(v_refs, h)) for h in range(N_HEADS)], axis=0)
    acc[...] = acc[...] * scale + pv

    @pl.when(step == pl.num_programs(1) - 1)
    def _():
        norm = jnp.concatenate([jnp.ones((sb_rows, 1), F32), 1.0 / l_scr[...]], axis=0)
        out = acc[...] * norm
        for h in range(N_HEADS):
            o_ref[:, head_lanes(h)] = head_rows(out, h)


def _sample_attn(layer, page_table, q, k_new, v_new, logf_new, cache_k, cache_v, cache_logf, n_new):
    db, n_pages = page_table.shape
    npg = PAGES_PER_STEP
    n_steps = n_pages // npg
    rows = N_HEADS * n_new
    fx_rows = H_FOX * n_new

    def page_of(b, s, pt, j):
        return pt[b, n_pages - (s + 1) * npg + j]

    per_row = lambda c: pl.BlockSpec((n_new, c), lambda b, s, pt: (b, 0))
    kv_specs = [pl.BlockSpec((None, None, N_HEADS, HEAD_DIM, PAGE_SIZE),
                             functools.partial(lambda b, s, pt, j: (layer, page_of(b, s, pt, j), 0, 0, 0), j=j))
                for j in range(npg)]
    lf_specs = [pl.BlockSpec((None, None, H_FOX, PAGE_SIZE),
                             functools.partial(lambda b, s, pt, j: (layer, page_of(b, s, pt, j), 0, 0), j=j))
                for j in range(npg)]
    grid_spec = pltpu.PrefetchScalarGridSpec(
        num_scalar_prefetch=1,
        grid=(db, n_steps),
        in_specs=[per_row(D_MODEL), per_row(D_MODEL), per_row(D_MODEL), per_row(H_FOX)]
                 + kv_specs + kv_specs + lf_specs,
        out_specs=per_row(D_MODEL),
        scratch_shapes=[pltpu.VMEM((N_HEADS, n_new, HEAD_DIM), F32),
                        pltpu.VMEM((rows, HEAD_DIM), F32),
                        pltpu.VMEM((fx_rows, 1), F32),
                        pltpu.VMEM((fx_rows, 1), F32),
                        pltpu.VMEM((rows - fx_rows, 1), F32),
                        pltpu.VMEM((H_FOX, 1), F32),
                        pltpu.VMEM((fx_rows, 1), F32),
                        pltpu.VMEM((npg * PAGE_SIZE, npg * PAGE_SIZE), BF16)],
    )
    return pl.pallas_call(
        functools.partial(_sample_body, n_new),
        grid_spec=grid_spec,
        out_shape=jax.ShapeDtypeStruct((db * n_new, D_MODEL), F32),
        compiler_params=_cparams("parallel", "arbitrary"),
        name="sample_attn",
    )(page_table, q, k_new, v_new, logf_new, *([cache_k] * npg), *([cache_v] * npg), *([cache_logf] * npg))


def _route(logits):
    rows = logits.shape[0]
    g = logits[:, :N_GROUPS]
    g_lane = lax.broadcasted_iota(jnp.int32, (rows, N_GROUPS), 1).astype(F32)
    g_max = jnp.max(g, axis=1, keepdims=True)
    grp = jnp.min(jnp.where(g == g_max, g_lane, float(N_GROUPS)), axis=1, keepdims=True)
    p_grp = 1.0 / jnp.sum(jnp.exp(g - g_max), axis=1, keepdims=True)
    e = logits[:, N_GROUPS:]
    lane = lax.broadcasted_iota(jnp.int32, (rows, N_EXPERTS), 1)
    e_lane = lane.astype(F32)
    in_group = (lane // EXPERTS_PER_GROUP).astype(F32) == grp
    m1 = jnp.where(in_group, e, -jnp.inf)
    v1 = jnp.max(m1, axis=1, keepdims=True)
    i1 = jnp.min(jnp.where(m1 == v1, e_lane, float(N_EXPERTS)), axis=1, keepdims=True)
    m2 = jnp.where(e_lane == i1, -jnp.inf, m1)
    v2 = jnp.max(m2, axis=1, keepdims=True)
    i2 = jnp.min(jnp.where(m2 == v2, e_lane, float(N_EXPERTS)), axis=1, keepdims=True)
    t = jnp.exp(v2 - v1)
    g1 = p_grp / (1.0 + t)
    return i1, i2, g1, g1 * t


def _merge_body(x_ref, o_ref, gate_ref, gsb_ref, gfx_ref, wo_ref, gffn_ref, wr_ref, br_ref,
                xmid_ref, h_ref, eid_ref, egate_ref):
    a_sb = (_rms_scale(o_ref[:, :W_SB]) * gsb_ref[...]).astype(BF16)
    a_fx = (_rms_scale(o_ref[:, W_SB:]) * gfx_ref[...] * jax.nn.sigmoid(gate_ref[...].astype(F32))).astype(BF16)
    y = (jnp.dot(a_sb, wo_ref[:W_SB, :], preferred_element_type=F32)
         + jnp.dot(a_fx, wo_ref[W_SB:, :], preferred_element_type=F32))
    x = x_ref[...] + y
    xmid_ref[...] = x
    h = _rms_scale(x) * gffn_ref[...]
    _store_row_tiled(h_ref, h)
    logits = jnp.dot(h, wr_ref[...], preferred_element_type=F32, precision=lax.Precision.HIGHEST) + br_ref[...]
    i1, i2, g1, g2 = _route(logits)
    eid_ref[:, 0:1] = i1.astype(jnp.int32)
    eid_ref[:, 1:2] = i2.astype(jnp.int32)
    egate_ref[:, 0:1] = g1
    egate_ref[:, 1:2] = g2


def _merge(x, o_parts, gate, g_sb, g_fx, w_out, g_ffn, w_route, b_route, tm):
    n = x.shape[0]
    row = lambda c: pl.BlockSpec((tm, c), lambda i: (i, 0))
    full = lambda a: pl.BlockSpec(a.shape, lambda i: (0,) * a.ndim)
    body = _merge_body
    if len(o_parts) == 2:
        def body(x_ref, osb_ref, ofx_ref, *rest):
            return _merge_body(x_ref, _Halves(osb_ref, ofx_ref), *rest)
    return pl.pallas_call(
        body,
        grid=(n // tm,),
        in_specs=[row(D_MODEL)] + [row(o.shape[1]) for o in o_parts]
                 + [row(W_FOX), full(g_sb), full(g_fx), full(w_out), full(g_ffn), full(w_route), full(b_route)],
        out_specs=[row(D_MODEL), pl.BlockSpec((tm * ROW_TILES, LANES), lambda i: (i, 0)), row(2), row(2)],
        out_shape=[jax.ShapeDtypeStruct((n, D_MODEL), F32),
                   jax.ShapeDtypeStruct((n * ROW_TILES, LANES), F32),
                   jax.ShapeDtypeStruct((n, 2), jnp.int32),
                   jax.ShapeDtypeStruct((n, 2), F32)],
        compiler_params=_cparams("parallel"),
        name="merge",
    )(x, *o_parts, gate, g_sb, g_fx, w_out, g_ffn, w_route, b_route)


class _Halves:
    def __init__(self, lo, hi):
        self.lo, self.hi = lo, hi

    def __getitem__(self, idx):
        rows, cols = idx
        if cols == slice(None, W_SB):
            return self.lo[rows, :]
        assert cols == slice(W_SB, None)
        return self.hi[rows, :]


def _expert_body(n_tok, be_ref, nv_ref, dst_ref, h_hbm, gate_ref, w1_ref, w3_ref, w2_ref, y_hbm,
                 xbuf, ybuf, gsem, ssem):
    i = pl.program_id(0)
    nb = pl.num_programs(0)
    slot = i % 2
    tile = lambda r: pl.ds(pl.multiple_of(r * ROW_TILES, ROW_TILES), ROW_TILES)

    def gather(b, sl):
        def one(r, c):
            d = dst_ref[b * MOE_BLOCK + r]
            tok = jnp.where(d >= n_tok, d - n_tok, d)
            pltpu.make_async_copy(h_hbm.at[tile(tok), :], xbuf.at[sl, tile(r), :], gsem.at[sl]).start()
            return c
        lax.fori_loop(0, nv_ref[b], one, 0)

    def gather_wait(b, sl):
        n = nv_ref[b] * ROW_TILES

        @pl.when(n > 0)
        def _():
            pltpu.make_async_copy(h_hbm.at[pl.ds(0, n), :], xbuf.at[sl, pl.ds(0, n), :], gsem.at[sl]).wait()

    def scatter(b, sl):
        def one(r, c):
            d = dst_ref[b * MOE_BLOCK + r]
            pltpu.make_async_copy(ybuf.at[sl, tile(r), :], y_hbm.at[tile(d), :], ssem.at[sl]).start()
            return c
        lax.fori_loop(0, nv_ref[b], one, 0)

    def scatter_wait(b, sl):
        n = nv_ref[b] * ROW_TILES

        @pl.when(n > 0)
        def _():
            pltpu.make_async_copy(ybuf.at[sl, pl.ds(0, n), :], y_hbm.at[pl.ds(0, n), :], ssem.at[sl]).wait()

    @pl.when(i == 0)
    def _():
        xbuf[...] = jnp.zeros_like(xbuf)
        gather(0, 0)

    @pl.when(i + 1 < nb)
    def _():
        gather(i + 1, 1 - slot)

    gather_wait(i, slot)

    @pl.when(i >= 2)
    def _():
        scatter_wait(i - 2, slot)

    @pl.when(nv_ref[i] > 0)
    def _():
        x = _load_row_tiled(xbuf.at[slot], MOE_BLOCK).astype(BF16)
        a = jnp.dot(x, w1_ref[...].astype(BF16), preferred_element_type=F32)
        b = jnp.dot(x, w3_ref[...].astype(BF16), preferred_element_type=F32)
        hmid = (a * jax.nn.sigmoid(a) * b).astype(BF16)
        y = jnp.dot(hmid, w2_ref[...].astype(BF16), preferred_element_type=F32) * gate_ref[...]
        _store_row_tiled(ybuf, y, (slot,))
        scatter(i, slot)

    @pl.when(i == nb - 1)
    def _():
        @pl.when(i >= 1)
        def _():
            scatter_wait(i - 1, 1 - slot)
        scatter_wait(i, slot)


def _experts(n_tok, block_exp, n_valid, dst_row, h_rt, slot_gate, layer, w1, w3, w2):
    n_blocks = block_exp.shape[0]
    buf = pltpu.VMEM((2, MOE_BLOCK * ROW_TILES, LANES), F32)
    expert = lambda i, be, nv, dst: (layer, be[i], 0, 0)
    grid_spec = pltpu.PrefetchScalarGridSpec(
        num_scalar_prefetch=3,
        grid=(n_blocks,),
        in_specs=[pl.BlockSpec(memory_space=pl.ANY),
                  pl.BlockSpec((MOE_BLOCK, 1), lambda i, be, nv, dst: (i, 0)),
                  pl.BlockSpec((None, None, D_MODEL, D_EXPERT), expert),
                  pl.BlockSpec((None, None, D_MODEL, D_EXPERT), expert),
                  pl.BlockSpec((None, None, D_EXPERT, D_MODEL), expert)],
        out_specs=pl.BlockSpec(memory_space=pl.ANY),
        scratch_shapes=[buf, buf, pltpu.SemaphoreType.DMA((2,)), pltpu.SemaphoreType.DMA((2,))],
    )
    return pl.pallas_call(
        functools.partial(_expert_body, n_tok),
        grid_spec=grid_spec,
        out_shape=jax.ShapeDtypeStruct((2 * n_tok * ROW_TILES, LANES), F32),
        compiler_params=_cparams("arbitrary"),
        name="experts",
    )(block_exp, n_valid, dst_row, h_rt, slot_gate, w1, w3, w2)


def _moe(h_rt, eid, gate, layer, w1, w3, w2):
    n_tok = eid.shape[0]
    n_asg = n_tok * 2
    flat_e = eid.reshape(n_asg)
    order = jnp.argsort(flat_e).astype(jnp.int32)
    counts = jnp.sum((flat_e[:, None] == jnp.arange(N_EXPERTS, dtype=jnp.int32)[None, :]).astype(jnp.int32), axis=0)
    padded = (counts + MOE_BLOCK - 1) // MOE_BLOCK * MOE_BLOCK
    ends_p = jnp.cumsum(padded)
    run_start = ends_p - padded
    sorted_start = jnp.cumsum(counts) - counts
    n_blocks = -(-n_asg // MOE_BLOCK) + N_EXPERTS
    block_start = jnp.arange(n_blocks, dtype=jnp.int32) * MOE_BLOCK
    block_exp = jnp.minimum(jnp.sum((ends_p[None, :] <= block_start[:, None]).astype(jnp.int32), axis=1),
                            N_EXPERTS - 1).astype(jnp.int32)
    block_off = block_start - run_start[block_exp]
    n_valid = jnp.clip(counts[block_exp] - block_off, 0, MOE_BLOCK).astype(jnp.int32)
    in_block = jnp.arange(MOE_BLOCK, dtype=jnp.int32)[None, :]
    valid = in_block < n_valid[:, None]
    pos = jnp.clip((sorted_start[block_exp] + block_off)[:, None] + in_block, 0, n_asg - 1)
    asg = order[pos.reshape(-1)]
    dst_row = jnp.where(valid.reshape(-1), (asg % 2) * n_tok + asg // 2, 0).astype(jnp.int32)
    slot_gate = jnp.where(valid.reshape(-1), gate.reshape(n_asg)[asg], 0.0)
    return _experts(n_tok, block_exp, n_valid, dst_row, h_rt, slot_gate[:, None], layer, w1, w3, w2)


def _final_body(x_ref, ya_ref, yb_ref, g_ref, o_ref):
    rows = x_ref.shape[0]
    x = x_ref[...] + (_load_row_tiled(ya_ref, rows) + _load_row_tiled(yb_ref, rows))
    o_ref[...] = _rms_scale(x) * g_ref[...]


def _final_norm(x, y2, g, tm):
    n = x.shape[0]
    nt = n // tm
    row = pl.BlockSpec((tm, D_MODEL), lambda i: (i, 0))
    return pl.pallas_call(
        _final_body,
        grid=(nt,),
        in_specs=[row,
                  pl.BlockSpec((tm * ROW_TILES, LANES), lambda i: (i, 0)),
                  pl.BlockSpec((tm * ROW_TILES, LANES), lambda i: (i + nt, 0)),
                  pl.BlockSpec(g.shape, lambda i: (0, 0))],
        out_specs=row,
        out_shape=jax.ShapeDtypeStruct((n, D_MODEL), F32),
        compiler_params=_cparams("parallel"),
        name="final_norm",
    )(x, y2, y2, g)


def _row_tile(n):
    for tm in (768, 512, 384, 256, 128, 64, 32, 16, 8):
        if n % tm == 0:
            return tm
    raise ValueError(f"row count {n} is not a multiple of 8")


def kernel(x_prompt, x_sample, cache_k, cache_v, cache_logf, page_table, meta_tokens, norm_mix_g, w_in, b_forget,
           norm_sb_g, norm_fox_g, w_out, norm_ffn_g, w_group, b_group, w_router, b_router, w_exp_gate, w_exp_up,
           w_exp_down, norm_final_g):
    bsz, seq, _ = x_prompt.shape
    db, n_new, _ = x_sample.shape
    depth = w_in.shape[0]
    t = seq + N_META
    n_p = bsz * t
    n_s = db * n_new
    assert (seq % ATTN_BLOCK) == 0 and page_table.shape[1] % PAGES_PER_STEP == 0

    meta = jnp.broadcast_to(meta_tokens[None].astype(x_prompt.dtype), (bsz, N_META, D_MODEL))
    xp = jnp.concatenate([meta, x_prompt], axis=1).reshape(n_p, D_MODEL)
    xs = x_sample.reshape(n_s, D_MODEL)
    ck_t = jnp.transpose(cache_k, (0, 1, 3, 4, 2))
    cv_t = jnp.transpose(cache_v, (0, 1, 3, 4, 2))
    clf_t = jnp.transpose(cache_logf, (0, 1, 3, 2))
    tm_p, tm_s = _row_tile(n_p), _row_tile(n_s)
    n_main = 3 * D_MODEL + W_FOX

    streams = {"p": (xp, None), "s": (xs, None)}
    outs = {name: {"k": [], "v": [], "lf": []} for name in streams}
    for l in range(depth):
        w_main = w_in[l, :, :n_main].astype(BF16)
        w_f = w_in[l, :, n_main:].astype(BF16)
        b_f = b_forget[l][None, :]
        g_mix = norm_mix_g[l][None, :]
        w_o = w_out[l].astype(BF16)
        w_route = jnp.concatenate([w_group[l], w_router[l]], axis=1)
        b_route = jnp.concatenate([b_group[l], b_router[l]])[None, :]
        new_streams = {}
        for name, tm in (("p", tm_p), ("s", tm_s)):
            x, y2 = streams[name]
            r = _inproj(x, y2, g_mix, w_main, w_f, b_f, tm)
            if y2 is not None:
                x, r = r[0], r[1:]
            q, k, v, gate, logf = r
            outs[name]["k"].append(k)
            outs[name]["v"].append(v)
            outs[name]["lf"].append(logf)
            if name == "p":
                q3, k3, v3 = (a.reshape(bsz, t, D_MODEL) for a in (q, k, v))
                ccol, crow_meta, crow = _forget_cumsum(logf.reshape(bsz, t, H_FOX))
                o_sb, o_fx = _prompt_attn(q3, k3, v3, ccol, crow_meta, crow)
                o_parts = (o_sb.reshape(n_p, W_SB), o_fx.reshape(n_p, W_FOX))
            else:
                o_parts = (_sample_attn(l, page_table, q, k, v, logf, ck_t, cv_t, clf_t, n_new),)
            x_mid, h_rt, eid, egate = _merge(x, o_parts, gate, norm_sb_g[l][None, :], norm_fox_g[l][None, :], w_o,
                                             norm_ffn_g[l][None, :], w_route, b_route, tm)
            new_streams[name] = (x_mid, _moe(h_rt, eid, egate, l, w_exp_gate, w_exp_up, w_exp_down))
        streams = new_streams

    g_fin = norm_final_g[None, :]
    y_prompt = _final_norm(*streams["p"], g_fin, tm_p).reshape(bsz, t, D_MODEL)[:, N_META:]
    y_sample = _final_norm(*streams["s"], g_fin, tm_s).reshape(db, n_new, D_MODEL)
    heads_p = (bsz, t, N_HEADS, HEAD_DIM)
    heads_s = (db, n_new, N_HEADS, HEAD_DIM)
    return (y_prompt, y_sample,
            jnp.stack([a.reshape(heads_p) for a in outs["p"]["k"]]),
            jnp.stack([a.reshape(heads_p) for a in outs["p"]["v"]]),
            jnp.stack([a.reshape(bsz, t, H_FOX) for a in outs["p"]["lf"]]),
            jnp.stack([a.reshape(heads_s) for a in outs["s"]["k"]]),
            jnp.stack([a.reshape(heads_s) for a in outs["s"]["v"]]),
            jnp.stack([a.reshape(db, n_new, H_FOX) for a in outs["s"]["lf"]]))
```

```python
import functools

import jax
import jax.numpy as jnp
from jax import lax
from jax.experimental import pallas as pl
from jax.experimental.pallas import tpu as pltpu

F32 = jnp.float32
BF16 = jnp.bfloat16

D_MODEL = 1024
HEAD_DIM = 64
H_SB = 8
H_FOX = 8
N_HEADS = H_SB + H_FOX
W_SB = H_SB * HEAD_DIM
W_FOX = H_FOX * HEAD_DIM
N_META = 16
N_GROUPS = 4
EXPERTS_PER_GROUP = 8
N_EXPERTS = N_GROUPS * EXPERTS_PER_GROUP
D_EXPERT = D_MODEL // 2
MOE_BLOCK = 256
RMS_EPS = 1e-6
PAGE_SIZE = 128
LANES = 128
SUBLANES = 8
ROW_TILES = D_MODEL // LANES
ATTN_BLOCK = 256
Q_TILE = 128
PAGES_PER_STEP = 8
NEG_BIG = -1e30
VMEM_LIMIT = 56 * 1024 * 1024


def _cparams(*sem):
    return pltpu.CompilerParams(dimension_semantics=sem, vmem_limit_bytes=VMEM_LIMIT)


def _log_sigmoid_fast(z):
    return jnp.minimum(z, 0.0) - jnp.log(1.0 + jnp.exp(-jnp.abs(z)))


def _log_sigmoid(x):
    return jnp.minimum(x, 0.0) - jnp.log1p(jnp.exp(-jnp.abs(x)))


def _rms_scale(x):
    return x * lax.rsqrt(jnp.mean(x * x, axis=-1, keepdims=True) + RMS_EPS)


def _dot_nt(a, b):
    return lax.dot_general(a, b, (((1,), (1,)), ((), ())), preferred_element_type=F32)


def _tri(n, kind):
    a = lax.broadcasted_iota(jnp.int32, (n, n), 0)
    b = lax.broadcasted_iota(jnp.int32, (n, n), 1)
    m = {"suffix": a > b, "prefix": b <= a, "eye": a == b}[kind]
    return jnp.where(m, 1.0, 0.0).astype(BF16)


def _pieces(x, parts):
    out = []
    rem = x
    for p in range(parts):
        piece = rem.astype(BF16)
        out.append(piece)
        if p + 1 < parts:
            rem = rem - piece.astype(F32)
    return out


def _split_dot(x, m01, parts):
    return sum(jnp.dot(p, m01, preferred_element_type=F32) for p in _pieces(x, parts))


def _split_dot_rhs(m01, x, parts):
    return sum(jnp.dot(m01, p, preferred_element_type=F32) for p in _pieces(x, parts))


def _transpose_small(x, parts=3):
    eye = _tri(x.shape[1], "eye")
    return sum(_dot_nt(eye, p) for p in _pieces(x, parts))


def _load_row_tiled(ref, rows):
    return jnp.concatenate([ref[pl.ds(s, rows, stride=ROW_TILES), :] for s in range(ROW_TILES)], axis=1)


def _store_row_tiled(ref, x, lead=()):
    rows = x.shape[0]
    for s in range(ROW_TILES):
        ref[lead + (pl.ds(s, rows, stride=ROW_TILES), slice(None))] = x[:, s * LANES:(s + 1) * LANES]


def _inproj_body(has_res, *refs):
    if has_res:
        x_ref, ya_ref, yb_ref, g_ref, w_ref, wf_ref, bf_ref, _, _, xo_ref, q_ref, k_ref, v_ref, gate_ref, logf_ref = refs
        rows = x_ref.shape[0]
        x = x_ref[...] + (_load_row_tiled(ya_ref, rows) + _load_row_tiled(yb_ref, rows))
        xo_ref[...] = x
    else:
        x_ref, g_ref, w_ref, wf_ref, bf_ref, q_ref, k_ref, v_ref, gate_ref, logf_ref = refs
        x = x_ref[...]
    h = (_rms_scale(x) * g_ref[...]).astype(BF16)
    half = D_MODEL // 2
    q_scale = HEAD_DIM ** -0.5
    for c in range(7):
        z = jnp.dot(h, w_ref[:, c * half:(c + 1) * half], preferred_element_type=F32)
        dst = pl.ds((c % 2) * half, half)
        if c < 2:
            q_ref[:, dst] = (z * q_scale).astype(BF16)
        elif c < 4:
            k_ref[:, dst] = z
        elif c < 6:
            v_ref[:, dst] = z
        else:
            gate_ref[...] = z.astype(BF16)
    zf = jnp.dot(h, wf_ref[...], preferred_element_type=F32) + bf_ref[...]
    logf_ref[...] = _log_sigmoid(zf)


def _inproj(x, prev, g, w_main, w_f, b_f, tm, layer, depth):
    n = x.shape[0]
    nt = n // tm
    row = lambda c: pl.BlockSpec((tm, c), lambda i: (i, 0))
    full = lambda a: pl.BlockSpec(a.shape, lambda i: (0,) * a.ndim)
    layer_row = pl.BlockSpec((None, tm, D_MODEL), lambda i: (layer, i, 0))
    out_shape = [jax.ShapeDtypeStruct((n, D_MODEL), BF16),
                 jax.ShapeDtypeStruct((depth, n, D_MODEL), F32),
                 jax.ShapeDtypeStruct((depth, n, D_MODEL), F32),
                 jax.ShapeDtypeStruct((n, W_FOX), BF16),
                 jax.ShapeDtypeStruct((n, H_FOX), F32)]
    out_specs = [row(D_MODEL), layer_row, layer_row, row(W_FOX), row(H_FOX)]
    in_specs = [row(D_MODEL)]
    args = [x]
    tail_specs, tail_args, aliases = [], [], {}
    if prev is not None:
        y2, k_all, v_all = prev
        out_shape = [jax.ShapeDtypeStruct((n, D_MODEL), F32)] + out_shape
        out_specs = [row(D_MODEL)] + out_specs
        in_specs += [pl.BlockSpec((tm * ROW_TILES, LANES), lambda i: (i, 0)),
                     pl.BlockSpec((tm * ROW_TILES, LANES), lambda i: (i + nt, 0))]
        args += [y2, y2]
        tail_specs = [pl.BlockSpec(memory_space=pl.ANY)] * 2
        tail_args = [k_all, v_all]
        first_alias = len(args) + 4
        aliases = {first_alias: 2, first_alias + 1: 3}
    return pl.pallas_call(
        functools.partial(_inproj_body, prev is not None),
        grid=(nt,),
        in_specs=in_specs + [full(g), full(w_main), full(w_f), full(b_f)] + tail_specs,
        out_specs=out_specs,
        out_shape=out_shape,
        input_output_aliases=aliases,
        compiler_params=_cparams("parallel"),
        name="inproj",
    )(*args, g, w_main, w_f, b_f, *tail_args)


def _forget_cumsum_body(lf_ref, ccol_ref, crow_meta_ref, crow_ref):
    t = lf_ref.shape[1]
    n_blocks = (t - N_META) // ATTN_BLOCK
    meta = pl.ds(0, N_META)
    c = _split_dot_rhs(_tri(N_META, "prefix"), lf_ref[0, meta, :], 3)
    ccol_ref[0, meta, :] = c
    crow_meta_ref[0] = _transpose_small(c)
    carry0 = c[N_META - 1:N_META, :]
    prefix = _tri(ATTN_BLOCK, "prefix")

    def block(i, carry):
        r0 = pl.multiple_of(N_META + i * ATTN_BLOCK, 16)
        rows = pl.ds(r0, ATTN_BLOCK)
        c = _split_dot_rhs(prefix, lf_ref[0, rows, :], 3) + carry
        ccol_ref[0, rows, :] = c
        crow_ref[0, :, pl.ds(pl.multiple_of(i * ATTN_BLOCK, ATTN_BLOCK), ATTN_BLOCK)] = _transpose_small(c)
        return c[ATTN_BLOCK - 1:ATTN_BLOCK, :]

    lax.fori_loop(0, n_blocks, block, carry0)


def _forget_cumsum(logf):
    b, t, _ = logf.shape
    return pl.pallas_call(
        _forget_cumsum_body,
        grid=(b,),
        in_specs=[pl.BlockSpec((1, t, H_FOX), lambda i: (i, 0, 0))],
        out_specs=[pl.BlockSpec((1, t, H_FOX), lambda i: (i, 0, 0)),
                   pl.BlockSpec((1, H_FOX, N_META), lambda i: (i, 0, 0)),
                   pl.BlockSpec((1, H_FOX, t - N_META), lambda i: (i, 0, 0))],
        out_shape=[jax.ShapeDtypeStruct((b, t, H_FOX), F32),
                   jax.ShapeDtypeStruct((b, H_FOX, N_META), F32),
                   jax.ShapeDtypeStruct((b, H_FOX, t - N_META), F32)],
        compiler_params=_cparams("parallel"),
        name="forget_cumsum",
    )(logf)


def _sb_tile(q, k, v, tail, diag, off=0):
    tq, tk = q.shape[0], k.shape[0]
    z = _dot_nt(q, k)
    ls = _log_sigmoid_fast(z)
    l1m = ls - z
    if diag:
        row = lax.broadcasted_iota(jnp.int32, (tq, tk), 0)
        col = lax.broadcasted_iota(jnp.int32, (tq, tk), 1)
        valid = col < row + off
        l1m = jnp.where(valid, l1m, 0.0)
    suffix = _split_dot(l1m, _tri(tk, "suffix"), 2)
    w = jnp.exp(ls + suffix + tail)
    if diag:
        w = jnp.where(valid, w, 0.0)
    o = jnp.dot(w.astype(BF16), v, preferred_element_type=F32)
    tail = tail + suffix[:, 0:1] + l1m[:, 0:1]
    return o, tail


def _fx_tile(q, k, v, cq, ck, m, l, acc, diag, off=0):
    tq, tk = q.shape[0], k.shape[0]
    z = _dot_nt(q, k) + (cq - ck)
    if diag:
        row = lax.broadcasted_iota(jnp.int32, (tq, tk), 0)
        col = lax.broadcasted_iota(jnp.int32, (tq, tk), 1)
        z = jnp.where(col <= row + off, z, -jnp.inf)
    m_new = jnp.maximum(m, jnp.max(z, axis=-1, keepdims=True))
    alpha = jnp.exp(m - m_new)
    p = jnp.exp(z - m_new)
    l = alpha * l + jnp.sum(p, axis=-1, keepdims=True)
    acc = alpha * acc + jnp.dot(p.astype(BF16), v, preferred_element_type=F32)
    return m_new, l, acc


def _prompt_attn_body(qsb_ref, ksb_ref, vsb_ref, qfx_ref, kfx_ref, vfx_ref, ccol_ref, crm_ref, crr_ref,
                      osb_ref, ofx_ref, qs, ks, vs, acc, col, u_scr, z_scr, zm_scr):
    u_scr[...] = _tri(ATTN_BLOCK, "suffix")
    for g, (qr, kr, vr) in enumerate(((qsb_ref, ksb_ref, vsb_ref), (qfx_ref, kfx_ref, vfx_ref))):
        for hh in range(2):
            lanes = slice(hh * HEAD_DIM, (hh + 1) * HEAD_DIM)
            qs[2 * g + hh] = qr[0, :, lanes]
            ks[2 * g + hh] = kr[0, :, lanes].astype(BF16)
            vs[2 * g + hh] = vr[0, :, lanes].astype(BF16)
    t = qsb_ref.shape[1]
    n_blocks = (t - N_META) // Q_TILE
    pair = pl.program_id(1)
    meta = pl.ds(0, N_META)
    head_lane = lax.broadcasted_iota(jnp.int32, (1, H_FOX), 1)

    def query_bias(rows, hh):
        sel = head_lane == pair * 2 + hh
        return jnp.sum(jnp.where(sel, ccol_ref[0, rows, :], 0.0), axis=1, keepdims=True)

    def key_bias_meta(hh):
        return crm_ref[0, pl.ds(pair * 2 + hh, 1), :]

    def fx_init(tq):
        return (jnp.full((tq, 1), NEG_BIG, F32), jnp.zeros((tq, 1), F32), jnp.zeros((tq, HEAD_DIM), F32))

    for hh in range(2):
        lanes = slice(hh * HEAD_DIM, (hh + 1) * HEAD_DIM)
        o_meta, _ = _sb_tile(qs[hh, meta, :], ks[hh, meta, :], vs[hh, meta, :], jnp.zeros((N_META, 1), F32), True)
        osb_ref[0, meta, lanes] = o_meta
        g = 2 + hh
        _, l, a = _fx_tile(qs[g, meta, :], ks[g, meta, :], vs[g, meta, :], query_bias(meta, hh), key_bias_meta(hh),
                           *fx_init(N_META), True)
        ofx_ref[0, meta, lanes] = a / l

    def scores(rows, keys, z_ref):
        for g in range(4):
            z_ref[g] = _dot_nt(qs[g, rows, :], ks[g, keys, :])

    def q_rows(qi):
        return pl.ds(pl.multiple_of(N_META + qi * Q_TILE, 16), Q_TILE)

    def key_tile(kj):
        return pl.ds(pl.multiple_of(N_META + kj * ATTN_BLOCK, 16), ATTN_BLOCK)

    def diag_tile(qi):
        return (qi * Q_TILE) // ATTN_BLOCK

    def tiles(rows, keys, ck_of, diag, first, z_ref, prefetch, off=0):
        tq = Q_TILE
        tk = keys.size
        z = [z_ref[g] for g in range(4)]
        prefetch()
        if diag:
            row = lax.broadcasted_iota(jnp.int32, (tq, tk), 0)
            col_id = lax.broadcasted_iota(jnp.int32, (tq, tk), 1)
            strict = col_id < row + off
            incl = col_id <= row + off
        ls = [_log_sigmoid_fast(z[hh]) for hh in range(2)]
        l1m = [ls[hh] - z[hh] for hh in range(2)]
        if diag:
            l1m = [jnp.where(strict, x, 0.0) for x in l1m]
        u = u_scr[...] if tk == ATTN_BLOCK else _tri(tk, "suffix")
        sums = jnp.dot(jnp.concatenate([x.astype(BF16) for x in l1m], axis=0), u, preferred_element_type=F32)
        suffix = [sums[hh * tq:(hh + 1) * tq] for hh in range(2)]
        fx = []
        for hh in range(2):
            g = 2 + hh
            m_old, l_old, a_old = fx_init(tq) if first else (col[2 + hh], col[4 + hh], acc[g])
            zf = z[g] + (col[6 + hh] - ck_of(hh))
            if diag:
                zf = jnp.where(incl, zf, -jnp.inf)
            m_new = jnp.maximum(m_old, jnp.max(zf, axis=-1, keepdims=True))
            alpha = jnp.exp(m_old - m_new)
            p = jnp.exp(zf - m_new)
            col[2 + hh] = m_new
            col[4 + hh] = alpha * l_old + jnp.sum(p, axis=-1, keepdims=True)
            fx.append((alpha * a_old, p.astype(BF16)))
        for hh in range(2):
            g = 2 + hh
            acc[g] = fx[hh][0] + jnp.dot(fx[hh][1], vs[g, keys, :], preferred_element_type=F32)
        for hh in range(2):
            tail = jnp.zeros((tq, 1), F32) if first else col[hh]
            w = jnp.exp(ls[hh] + suffix[hh] + tail)
            if diag:
                w = jnp.where(strict, w, 0.0)
            o = jnp.dot(w.astype(BF16), vs[hh, keys, :], preferred_element_type=F32)
            acc[hh] = o if first else acc[hh] + o
            col[hh] = tail + suffix[hh][:, 0:1] + l1m[hh][:, 0:1]

    def block_scores(qi):
        scores(q_rows(qi), key_tile(diag_tile(qi)), z_scr)
        scores(q_rows(qi), meta, zm_scr)

    def q_block(qi, carry):
        rows = q_rows(qi)
        for hh in range(2):
            col[6 + hh] = query_bias(rows, hh)

        def key_bias(kj):
            c0 = pl.multiple_of(kj * ATTN_BLOCK, ATTN_BLOCK)
            return lambda hh: crr_ref[0, pl.ds(pair * 2 + hh, 1), pl.ds(c0, ATTN_BLOCK)]

        def next_scores(kj):
            return lambda: scores(rows, key_tile(jnp.maximum(kj - 1, 0)), z_scr)

        kd = diag_tile(qi)
        tiles(rows, key_tile(kd), key_bias(kd), True, True, z_scr, next_scores(kd),
              qi * Q_TILE - kd * ATTN_BLOCK)

        def k_block(step, c):
            kj = kd - 1 - step
            tiles(rows, key_tile(kj), key_bias(kj), False, False, z_scr, next_scores(kj))
            return c

        lax.fori_loop(0, kd, k_block, 0)
        tiles(rows, meta, key_bias_meta, False, False, zm_scr,
              lambda: block_scores(jnp.minimum(qi + 1, n_blocks - 1)))
        for hh in range(2):
            lanes = slice(hh * HEAD_DIM, (hh + 1) * HEAD_DIM)
            osb_ref[0, rows, lanes] = acc[hh]
            ofx_ref[0, rows, lanes] = acc[2 + hh] / col[4 + hh]
        return carry

    block_scores(0)
    lax.fori_loop(0, n_blocks, q_block, 0)


def _prompt_attn(layer, q, k, v, ccol, crow_meta, crow):
    b, t, _ = q.shape
    n_pairs = H_SB // 2
    sb_col = pl.BlockSpec((1, t, LANES), lambda i, j: (i, 0, j))
    fx_col = pl.BlockSpec((1, t, LANES), lambda i, j: (i, 0, j + n_pairs))
    sb_kv = pl.BlockSpec((None, 1, t, LANES), lambda i, j: (layer, i, 0, j))
    fx_kv = pl.BlockSpec((None, 1, t, LANES), lambda i, j: (layer, i, 0, j + n_pairs))
    whole = lambda a: pl.BlockSpec((1,) + a.shape[1:], lambda i, j: (i, 0, 0))
    return pl.pallas_call(
        _prompt_attn_body,
        grid=(b, n_pairs),
        in_specs=[sb_col, sb_kv, sb_kv, fx_col, fx_kv, fx_kv, whole(ccol), whole(crow_meta), whole(crow)],
        out_specs=[sb_col, sb_col],
        out_shape=[jax.ShapeDtypeStruct((b, t, W_SB), F32), jax.ShapeDtypeStruct((b, t, W_FOX), F32)],
        scratch_shapes=[pltpu.VMEM((4, t, HEAD_DIM), BF16)] * 3
                       + [pltpu.VMEM((4, Q_TILE, HEAD_DIM), F32), pltpu.VMEM((8, Q_TILE, 1), F32),
                          pltpu.VMEM((ATTN_BLOCK, ATTN_BLOCK), BF16),
                          pltpu.VMEM((4, Q_TILE, ATTN_BLOCK), F32), pltpu.VMEM((4, Q_TILE, N_META), F32)],
        compiler_params=_cparams("parallel", "parallel"),
        name="prompt_attn",
    )(q, k, v, q, k, v, ccol, crow_meta, crow)


def _sample_body(n_new, pt_ref, q_ref, kn_ref, vn_ref, lfn_ref, *refs):
    npg = PAGES_PER_STEP
    k_refs = refs[0:npg]
    v_refs = refs[npg:2 * npg]
    lf_refs = refs[2 * npg:3 * npg]
    o_ref = refs[3 * npg]
    q_scr, acc, m_scr, l_scr, tail_scr, rc_scr, cq_scr, u_scr = refs[3 * npg + 1:]
    sb_rows = H_SB * n_new
    step = pl.program_id(1)
    tk = npg * PAGE_SIZE
    head_rows = lambda x, h: x[h * n_new:(h + 1) * n_new]
    head_lanes = lambda h: slice(h * HEAD_DIM, (h + 1) * HEAD_DIM)

    @pl.when(step == 0)
    def _():
        for h in range(N_HEADS):
            q_scr[h] = q_ref[:, head_lanes(h)].astype(F32)
        s_new = jnp.concatenate([_dot_nt(q_scr[h], kn_ref[:, head_lanes(h)]) for h in range(N_HEADS)], axis=0)
        qi = lax.broadcasted_iota(jnp.int32, (N_HEADS * n_new, n_new), 0) % n_new
        ki = lax.broadcasted_iota(jnp.int32, (N_HEADS * n_new, n_new), 1)
        z = s_new[:sb_rows]
        strict = (ki < qi)[:sb_rows]
        ls = _log_sigmoid_fast(z)
        l1m = jnp.where(strict, ls - z, 0.0)
        suffix = _split_dot(l1m, _tri(n_new, "suffix"), 3)
        w_sb = jnp.where(strict, jnp.exp(ls + suffix), 0.0)
        tail_scr[...] = jnp.sum(l1m, axis=-1, keepdims=True)
        cn = _split_dot_rhs(_tri(n_new, "prefix"), lfn_ref[...], 3)
        cn_t = _transpose_small(cn)
        cq = jnp.concatenate([cn[:, h:h + 1] for h in range(H_FOX)], axis=0)
        ck = jnp.concatenate([jnp.broadcast_to(cn_t[h:h + 1, :], (n_new, n_new)) for h in range(H_FOX)], axis=0)
        cq_scr[...] = cq
        zf = jnp.where((ki <= qi)[sb_rows:], s_new[sb_rows:] + (cq - ck), -jnp.inf)
        m0 = jnp.max(zf, axis=-1, keepdims=True)
        p = jnp.exp(zf - m0)
        m_scr[...] = m0
        l_scr[...] = jnp.sum(p, axis=-1, keepdims=True)
        rc_scr[...] = jnp.zeros_like(rc_scr)
        u_scr[...] = _tri(tk, "suffix")
        pw = jnp.concatenate([w_sb, p], axis=0)
        acc[...] = jnp.concatenate(
            [jnp.dot(head_rows(pw, h), vn_ref[:, head_lanes(h)], preferred_element_type=F32)
             for h in range(N_HEADS)], axis=0)

    def head_t(refs_, h):
        return jnp.concatenate([r[h].astype(BF16) for r in refs_], axis=1)

    s = jnp.concatenate([jnp.dot(q_scr[h].astype(BF16), head_t(k_refs, h), preferred_element_type=F32)
                         for h in range(N_HEADS)], axis=0)
    u = u_scr[...]
    z = s[:sb_rows]
    ls = _log_sigmoid_fast(z)
    l1m = ls - z
    suffix = _split_dot(l1m, u, 2)
    tail = tail_scr[...]
    w_sb = jnp.exp(ls + suffix + tail)
    tail_scr[...] = tail + suffix[:, 0:1] + l1m[:, 0:1]
    plf_t = jnp.concatenate([lf_refs[j][...] for j in range(npg)], axis=1)
    rc_prev = rc_scr[...]
    rc = _split_dot(plf_t, u, 3) + rc_prev
    rc_scr[...] = rc_prev + jnp.sum(plf_t, axis=-1, keepdims=True)
    bias = jnp.concatenate([jnp.broadcast_to(rc[h:h + 1, :], (n_new, tk)) for h in range(H_FOX)], axis=0)
    zf = s[sb_rows:] + (cq_scr[...] + bias)
    m_prev = m_scr[...]
    m_new = jnp.maximum(m_prev, jnp.max(zf, axis=-1, keepdims=True))
    alpha = jnp.exp(m_prev - m_new)
    p = jnp.exp(zf - m_new)
    m_scr[...] = m_new
    l_scr[...] = alpha * l_scr[...] + jnp.sum(p, axis=-1, keepdims=True)
    pw = jnp.concatenate([w_sb, p], axis=0)
    scale = jnp.concatenate([jnp.ones((sb_rows, 1), F32), alpha], axis=0)
    pv = jnp.concatenate(
        [_dot_nt(head_rows(pw, h).astype(BF16), head_t(v_refs, h)) for h in range(N_HEADS)], axis=0)
    acc[...] = acc[...] * scale + pv

    @pl.when(step == pl.num_programs(1) - 1)
    def _():
        norm = jnp.concatenate([jnp.ones((sb_rows, 1), F32), 1.0 / l_scr[...]], axis=0)
        out = acc[...] * norm
        for h in range(N_HEADS):
            o_ref[:, head_lanes(h)] = head_rows(out, h)


def _sample_attn(layer, page_table, q, k_new, v_new, logf_new, cache_k, cache_v, cache_logf, n_new):
    db, n_pages = page_table.shape
    npg = PAGES_PER_STEP
    n_steps = n_pages // npg
    rows = N_HEADS * n_new
    fx_rows = H_FOX * n_new

    def page_of(b, s, pt, j):
        return pt[b, n_pages - (s + 1) * npg + j]

    per_row = lambda c: pl.BlockSpec((n_new, c), lambda b, s, pt: (b, 0))
    new_kv = pl.BlockSpec((None, n_new, D_MODEL), lambda b, s, pt: (layer, b, 0))
    kv_specs =[pl.BlockSpec((None, None, N_HEADS, HEAD_DIM, PAGE_SIZE),
                             functools.partial(lambda b, s, pt, j: (layer, page_of(b, s, pt, j), 0, 0, 0), j=j))
                for j in range(npg)]
    lf_specs = [pl.BlockSpec((None, None, H_FOX, PAGE_SIZE),
                             functools.partial(lambda b, s, pt, j: (layer, page_of(b, s, pt, j), 0, 0), j=j))
                for j in range(npg)]
    grid_spec = pltpu.PrefetchScalarGridSpec(
        num_scalar_prefetch=1,
        grid=(db, n_steps),
        in_specs=[per_row(D_MODEL), new_kv, new_kv, per_row(H_FOX)] + kv_specs + kv_specs + lf_specs,
        out_specs=per_row(D_MODEL),
        scratch_shapes=[pltpu.VMEM((N_HEADS, n_new, HEAD_DIM), F32),
                        pltpu.VMEM((rows, HEAD_DIM), F32),
                        pltpu.VMEM((fx_rows, 1), F32),
                        pltpu.VMEM((fx_rows, 1), F32),
                        pltpu.VMEM((rows - fx_rows, 1), F32),
                        pltpu.VMEM((H_FOX, 1), F32),
                        pltpu.VMEM((fx_rows, 1), F32),
                        pltpu.VMEM((npg * PAGE_SIZE, npg * PAGE_SIZE), BF16)],
    )
    return pl.pallas_call(
        functools.partial(_sample_body, n_new),
        grid_spec=grid_spec,
        out_shape=jax.ShapeDtypeStruct((db * n_new, D_MODEL), F32),
        compiler_params=_cparams("parallel", "arbitrary"),
        name="sample_attn",
    )(page_table, q, k_new, v_new, logf_new, *([cache_k] * npg), *([cache_v] * npg), *([cache_logf] * npg))


def _route(logits):
    rows = logits.shape[0]
    g = logits[:, :N_GROUPS]
    g_lane = lax.broadcasted_iota(jnp.int32, (rows, N_GROUPS), 1).astype(F32)
    g_max = jnp.max(g, axis=1, keepdims=True)
    grp = jnp.min(jnp.where(g == g_max, g_lane, float(N_GROUPS)), axis=1, keepdims=True)
    p_grp = 1.0 / jnp.sum(jnp.exp(g - g_max), axis=1, keepdims=True)
    e = logits[:, N_GROUPS:]
    lane = lax.broadcasted_iota(jnp.int32, (rows, N_EXPERTS), 1)
    e_lane = lane.astype(F32)
    in_group = (lane // EXPERTS_PER_GROUP).astype(F32) == grp
    m1 = jnp.where(in_group, e, -jnp.inf)
    v1 = jnp.max(m1, axis=1, keepdims=True)
    i1 = jnp.min(jnp.where(m1 == v1, e_lane, float(N_EXPERTS)), axis=1, keepdims=True)
    m2 = jnp.where(e_lane == i1, -jnp.inf, m1)
    v2 = jnp.max(m2, axis=1, keepdims=True)
    i2 = jnp.min(jnp.where(m2 == v2, e_lane, float(N_EXPERTS)), axis=1, keepdims=True)
    t = jnp.exp(v2 - v1)
    g1 = p_grp / (1.0 + t)
    return i1, i2, g1, g1 * t


def _merge_body(x_ref, o_ref, gate_ref, gsb_ref, gfx_ref, wo_ref, gffn_ref, wr_ref, br_ref,
                xmid_ref, h_ref, eid_ref, egate_ref):
    a_sb = (_rms_scale(o_ref[:, :W_SB]) * gsb_ref[...]).astype(BF16)
    a_fx = (_rms_scale(o_ref[:, W_SB:]) * gfx_ref[...] * jax.nn.sigmoid(gate_ref[...].astype(F32))).astype(BF16)
    y = (jnp.dot(a_sb, wo_ref[:W_SB, :], preferred_element_type=F32)
         + jnp.dot(a_fx, wo_ref[W_SB:, :], preferred_element_type=F32))
    x = x_ref[...] + y
    xmid_ref[...] = x
    h = _rms_scale(x) * gffn_ref[...]
    _store_row_tiled(h_ref, h)
    logits = jnp.dot(h, wr_ref[...], preferred_element_type=F32, precision=lax.Precision.HIGHEST) + br_ref[...]
    i1, i2, g1, g2 = _route(logits)
    eid_ref[:, 0:1] = i1.astype(jnp.int32)
    eid_ref[:, 1:2] = i2.astype(jnp.int32)
    egate_ref[:, 0:1] = g1
    egate_ref[:, 1:2] = g2


def _merge(x, o_parts, gate, g_sb, g_fx, w_out, g_ffn, w_route, b_route, tm):
    n = x.shape[0]
    row = lambda c: pl.BlockSpec((tm, c), lambda i: (i, 0))
    full = lambda a: pl.BlockSpec(a.shape, lambda i: (0,) * a.ndim)
    body = _merge_body
    if len(o_parts) == 2:
        def body(x_ref, osb_ref, ofx_ref, *rest):
            return _merge_body(x_ref, _Halves(osb_ref, ofx_ref), *rest)
    return pl.pallas_call(
        body,
        grid=(n // tm,),
        in_specs=[row(D_MODEL)] + [row(o.shape[1]) for o in o_parts]
                 + [row(W_FOX), full(g_sb), full(g_fx), full(w_out), full(g_ffn), full(w_route), full(b_route)],
        out_specs=[row(D_MODEL), pl.BlockSpec((tm * ROW_TILES, LANES), lambda i: (i, 0)), row(2), row(2)],
        out_shape=[jax.ShapeDtypeStruct((n, D_MODEL), F32),
                   jax.ShapeDtypeStruct((n * ROW_TILES, LANES), F32),
                   jax.ShapeDtypeStruct((n, 2), jnp.int32),
                   jax.ShapeDtypeStruct((n, 2), F32)],
        compiler_params=_cparams("parallel"),
        name="merge",
    )(x, *o_parts, gate, g_sb, g_fx, w_out, g_ffn, w_route, b_route)


class _Halves:
    def __init__(self, lo, hi):
        self.lo, self.hi = lo, hi

    def __getitem__(self, idx):
        rows, cols = idx
        if cols == slice(None, W_SB):
            return self.lo[rows, :]
        assert cols == slice(W_SB, None)
        return self.hi[rows, :]


def _expert_body(n_tok, be_ref, nv_ref, dst_ref, h_hbm, gate_ref, w1_ref, w3_ref, w2_ref, y_hbm,
                 xbuf, ybuf, gsem, ssem):
    i = pl.program_id(0)
    nb = pl.num_programs(0)
    slot = i % 2
    tile = lambda r: pl.ds(pl.multiple_of(r * ROW_TILES, ROW_TILES), ROW_TILES)

    def gather(b, sl):
        def one(r, c):
            d = dst_ref[b * MOE_BLOCK + r]
            tok = jnp.where(d >= n_tok, d - n_tok, d)
            pltpu.make_async_copy(h_hbm.at[tile(tok), :], xbuf.at[sl, tile(r), :], gsem.at[sl]).start()
            return c
        lax.fori_loop(0, nv_ref[b], one, 0)

    def gather_wait(b, sl):
        n = nv_ref[b] * ROW_TILES

        @pl.when(n > 0)
        def _():
            pltpu.make_async_copy(h_hbm.at[pl.ds(0, n), :], xbuf.at[sl, pl.ds(0, n), :], gsem.at[sl]).wait()

    def scatter(b, sl):
        def one(r, c):
            d = dst_ref[b * MOE_BLOCK + r]
            pltpu.make_async_copy(ybuf.at[sl, tile(r), :], y_hbm.at[tile(d), :], ssem.at[sl]).start()
            return c
        lax.fori_loop(0, nv_ref[b], one, 0)

    def scatter_wait(b, sl):
        n = nv_ref[b] * ROW_TILES

        @pl.when(n > 0)
        def _():
            pltpu.make_async_copy(ybuf.at[sl, pl.ds(0, n), :], y_hbm.at[pl.ds(0, n), :], ssem.at[sl]).wait()

    @pl.when(i == 0)
    def _():
        xbuf[...] = jnp.zeros_like(xbuf)
        gather(0, 0)

    @pl.when(i + 1 < nb)
    def _():
        gather(i + 1, 1 - slot)

    gather_wait(i, slot)

    @pl.when(i >= 2)
    def _():
        scatter_wait(i - 2, slot)

    @pl.when(nv_ref[i] > 0)
    def _():
        x = _load_row_tiled(xbuf.at[slot], MOE_BLOCK).astype(BF16)
        a = jnp.dot(x, w1_ref[...].astype(BF16), preferred_element_type=F32)
        b = jnp.dot(x, w3_ref[...].astype(BF16), preferred_element_type=F32)
        hmid = (a * jax.nn.sigmoid(a) * b).astype(BF16)
        y = jnp.dot(hmid, w2_ref[...].astype(BF16), preferred_element_type=F32) * gate_ref[...]
        _store_row_tiled(ybuf, y, (slot,))
        scatter(i, slot)

    @pl.when(i == nb - 1)
    def _():
        @pl.when(i >= 1)
        def _():
            scatter_wait(i - 1, 1 - slot)
        scatter_wait(i, slot)


def _experts(n_tok, block_exp, n_valid, dst_row, h_rt, slot_gate, layer, w1, w3, w2):
    n_blocks = block_exp.shape[0]
    buf = pltpu.VMEM((2, MOE_BLOCK * ROW_TILES, LANES), F32)
    expert = lambda i, be, nv, dst: (layer, be[i], 0, 0)
    grid_spec = pltpu.PrefetchScalarGridSpec(
        num_scalar_prefetch=3,
        grid=(n_blocks,),
        in_specs=[pl.BlockSpec(memory_space=pl.ANY),
                  pl.BlockSpec((MOE_BLOCK, 1), lambda i, be, nv, dst: (i, 0)),
                  pl.BlockSpec((None, None, D_MODEL, D_EXPERT), expert),
                  pl.BlockSpec((None, None, D_MODEL, D_EXPERT), expert),
                  pl.BlockSpec((None, None, D_EXPERT, D_MODEL), expert)],
        out_specs=pl.BlockSpec(memory_space=pl.ANY),
        scratch_shapes=[buf, buf, pltpu.SemaphoreType.DMA((2,)), pltpu.SemaphoreType.DMA((2,))],
    )
    return pl.pallas_call(
        functools.partial(_expert_body, n_tok),
        grid_spec=grid_spec,
        out_shape=jax.ShapeDtypeStruct((2 * n_tok * ROW_TILES, LANES), F32),
        compiler_params=_cparams("arbitrary"),
        name="experts",
    )(block_exp, n_valid, dst_row, h_rt, slot_gate, w1, w3, w2)


def _moe(h_rt, eid, gate, layer, w1, w3, w2):
    n_tok = eid.shape[0]
    n_asg = n_tok * 2
    flat_e = eid.reshape(n_asg)
    order = jnp.argsort(flat_e).astype(jnp.int32)
    counts = jnp.sum((flat_e[:, None] == jnp.arange(N_EXPERTS, dtype=jnp.int32)[None, :]).astype(jnp.int32), axis=0)
    padded = (counts + MOE_BLOCK - 1) // MOE_BLOCK * MOE_BLOCK
    ends_p = jnp.cumsum(padded)
    run_start = ends_p - padded
    sorted_start = jnp.cumsum(counts) - counts
    n_blocks = -(-n_asg // MOE_BLOCK) + N_EXPERTS
    block_start = jnp.arange(n_blocks, dtype=jnp.int32) * MOE_BLOCK
    block_exp = jnp.minimum(jnp.sum((ends_p[None, :] <= block_start[:, None]).astype(jnp.int32), axis=1),
                            N_EXPERTS - 1).astype(jnp.int32)
    block_off = block_start - run_start[block_exp]
    n_valid = jnp.clip(counts[block_exp] - block_off, 0, MOE_BLOCK).astype(jnp.int32)
    in_block = jnp.arange(MOE_BLOCK, dtype=jnp.int32)[None, :]
    valid = in_block < n_valid[:, None]
    pos = jnp.clip((sorted_start[block_exp] + block_off)[:, None] + in_block, 0, n_asg - 1)
    asg = order[pos.reshape(-1)]
    dst_row = jnp.where(valid.reshape(-1), (asg % 2) * n_tok + asg // 2, 0).astype(jnp.int32)
    slot_gate = jnp.where(valid.reshape(-1), gate.reshape(n_asg)[asg], 0.0)
    return _experts(n_tok, block_exp, n_valid, dst_row, h_rt, slot_gate[:, None], layer, w1, w3, w2)


def _final_body(x_ref, ya_ref, yb_ref, g_ref, o_ref):
    rows = x_ref.shape[0]
    x = x_ref[...] + (_load_row_tiled(ya_ref, rows) + _load_row_tiled(yb_ref, rows))
    o_ref[...] = _rms_scale(x) * g_ref[...]


def _final_norm(x, y2, g, tm):
    n = x.shape[0]
    nt = n // tm
    row = pl.BlockSpec((tm, D_MODEL), lambda i: (i, 0))
    return pl.pallas_call(
        _final_body,
        grid=(nt,),
        in_specs=[row,
                  pl.BlockSpec((tm * ROW_TILES, LANES), lambda i: (i, 0)),
                  pl.BlockSpec((tm * ROW_TILES, LANES), lambda i: (i + nt, 0)),
                  pl.BlockSpec(g.shape, lambda i: (0, 0))],
        out_specs=row,
        out_shape=jax.ShapeDtypeStruct((n, D_MODEL), F32),
        compiler_params=_cparams("parallel"),
        name="final_norm",
    )(x, y2, y2, g)


def _row_tile(n):
    for tm in (768, 512, 384, 256, 128, 64, 32, 16, 8):
        if n % tm == 0:
            return tm
    raise ValueError(f"row count {n} is not a multiple of 8")


def kernel(x_prompt, x_sample, cache_k, cache_v, cache_logf, page_table, meta_tokens, norm_mix_g, w_in, b_forget,
           norm_sb_g, norm_fox_g, w_out, norm_ffn_g, w_group, b_group, w_router, b_router, w_exp_gate, w_exp_up,
           w_exp_down, norm_final_g):
    bsz, seq, _ = x_prompt.shape
    db, n_new, _ = x_sample.shape
    depth = w_in.shape[0]
    t = seq + N_META
    n_p = bsz * t
    n_s = db * n_new
    assert (seq % ATTN_BLOCK) == 0 and page_table.shape[1] % PAGES_PER_STEP == 0

    meta = jnp.broadcast_to(meta_tokens[None].astype(x_prompt.dtype), (bsz, N_META, D_MODEL))
    xp = jnp.concatenate([meta, x_prompt], axis=1).reshape(n_p, D_MODEL)
    xs = x_sample.reshape(n_s, D_MODEL)
    ck_t = jnp.transpose(cache_k, (0, 1, 3, 4, 2))
    cv_t = jnp.transpose(cache_v, (0, 1, 3, 4, 2))
    clf_t = jnp.transpose(cache_logf, (0, 1, 3, 2))
    tm_p, tm_s = _row_tile(n_p), _row_tile(n_s)
    n_main = 3 * D_MODEL + W_FOX

    streams = {"p": (xp, None, None, None), "s": (xs, None, None, None)}
    logfs = {name: [] for name in streams}
    for l in range(depth):
        w_main = w_in[l, :, :n_main].astype(BF16)
        w_f = w_in[l, :, n_main:].astype(BF16)
        b_f = b_forget[l][None, :]
        g_mix = norm_mix_g[l][None, :]
        w_o = w_out[l].astype(BF16)
        w_route = jnp.concatenate([w_group[l], w_router[l]], axis=1)
        b_route = jnp.concatenate([b_group[l], b_router[l]])[None, :]
        new_streams = {}
        for name, tm in (("p", tm_p), ("s", tm_s)):
            x, y2, k_all, v_all = streams[name]
            prev = None if y2 is None else (y2, k_all, v_all)
            r = _inproj(x, prev, g_mix, w_main, w_f, b_f, tm, l, depth)
            if prev is not None:
                x, r = r[0], r[1:]
            q, k_all, v_all, gate, logf = r
            logfs[name].append(logf)
            if name == "p":
                k4, v4 = (a.reshape(depth, bsz, t, D_MODEL) for a in (k_all, v_all))
                ccol, crow_meta, crow = _forget_cumsum(logf.reshape(bsz, t, H_FOX))
                o_sb, o_fx = _prompt_attn(l, q.reshape(bsz, t, D_MODEL), k4, v4, ccol, crow_meta, crow)
                o_parts = (o_sb.reshape(n_p, W_SB), o_fx.reshape(n_p, W_FOX))
            else:
                o_parts = (_sample_attn(l, page_table, q, k_all, v_all, logf, ck_t, cv_t, clf_t, n_new),)
            x_mid, h_rt, eid, egate = _merge(x, o_parts, gate, norm_sb_g[l][None, :], norm_fox_g[l][None, :], w_o,
                                             norm_ffn_g[l][None, :], w_route, b_route, tm)
            y2 = _moe(h_rt, eid, egate, l, w_exp_gate, w_exp_up, w_exp_down)
            new_streams[name] = (x_mid, y2, k_all, v_all)
        streams = new_streams

    g_fin = norm_final_g[None, :]
    y_prompt = _final_norm(*streams["p"][:2], g_fin, tm_p).reshape(bsz, t, D_MODEL)[:, N_META:]
    y_sample = _final_norm(*streams["s"][:2], g_fin, tm_s).reshape(db, n_new, D_MODEL)
    heads_s = (depth, db, n_new, N_HEADS, HEAD_DIM)

    def heads_p(a):
        a = jnp.transpose(a.reshape(depth, bsz, t, D_MODEL), (0, 1, 3, 2))
        return jnp.transpose(a.reshape(depth, bsz, N_HEADS, HEAD_DIM, t), (0, 1, 4, 2, 3))

    return (y_prompt, y_sample,
            heads_p(streams["p"][2]),
            heads_p(streams["p"][3]),
            jnp.stack([a.reshape(bsz, t, H_FOX) for a in logfs["p"]]),
            streams["s"][2].reshape(heads_s),
            streams["s"][3].reshape(heads_s),
            jnp.stack([a.reshape(db, n_new, H_FOX) for a in logfs["s"]]))
```

```python
import functools

import jax
import jax.numpy as jnp
from jax import lax
from jax.experimental import pallas as pl
from jax.experimental.pallas import tpu as pltpu

F32 = jnp.float32
BF16 = jnp.bfloat16

D_MODEL = 1024
HEAD_DIM = 64
H_SB = 8
H_FOX = 8
N_HEADS = H_SB + H_FOX
W_SB = H_SB * HEAD_DIM
W_FOX = H_FOX * HEAD_DIM
N_META = 16
N_GROUPS = 4
EXPERTS_PER_GROUP = 8
N_EXPERTS = N_GROUPS * EXPERTS_PER_GROUP
D_EXPERT = D_MODEL // 2
MOE_BLOCK = 256
RMS_EPS = 1e-6
PAGE_SIZE = 128
LANES = 128
SUBLANES = 8
ROW_TILES = D_MODEL // LANES
ATTN_BLOCK = 512
Q_TILE = 128
PAGES_PER_STEP = 8
NEG_BIG = -1e30
VMEM_LIMIT = 56 * 1024 * 1024


def _cparams(*sem):
    return pltpu.CompilerParams(dimension_semantics=sem, vmem_limit_bytes=VMEM_LIMIT)


def _log_sigmoid_fast(z):
    return jnp.minimum(z, 0.0) - jnp.log(1.0 + jnp.exp(-jnp.abs(z)))


def _log_sigmoid(x):
    return jnp.minimum(x, 0.0) - jnp.log1p(jnp.exp(-jnp.abs(x)))


def _rms_scale(x):
    return x * lax.rsqrt(jnp.mean(x * x, axis=-1, keepdims=True) + RMS_EPS)


def _dot_nt(a, b):
    return lax.dot_general(a, b, (((1,), (1,)), ((), ())), preferred_element_type=F32)


def _tri(n, kind):
    a = lax.broadcasted_iota(jnp.int32, (n, n), 0)
    b = lax.broadcasted_iota(jnp.int32, (n, n), 1)
    m = {"suffix": a > b, "prefix": b <= a, "eye": a == b}[kind]
    return jnp.where(m, 1.0, 0.0).astype(BF16)


def _pieces(x, parts):
    out = []
    rem = x
    for p in range(parts):
        piece = rem.astype(BF16)
        out.append(piece)
        if p + 1 < parts:
            rem = rem - piece.astype(F32)
    return out


def _split_dot(x, m01, parts):
    return sum(jnp.dot(p, m01, preferred_element_type=F32) for p in _pieces(x, parts))


def _split_dot_rhs(m01, x, parts):
    return sum(jnp.dot(m01, p, preferred_element_type=F32) for p in _pieces(x, parts))


def _transpose_small(x, parts=3):
    eye = _tri(x.shape[1], "eye")
    return sum(_dot_nt(eye, p) for p in _pieces(x, parts))


def _load_row_tiled(ref, rows):
    return jnp.concatenate([ref[pl.ds(s, rows, stride=ROW_TILES), :] for s in range(ROW_TILES)], axis=1)


def _store_row_tiled(ref, x, lead=()):
    rows = x.shape[0]
    for s in range(ROW_TILES):
        ref[lead + (pl.ds(s, rows, stride=ROW_TILES), slice(None))] = x[:, s * LANES:(s + 1) * LANES]


def _inproj_body(has_res, *refs):
    if has_res:
        x_ref, ya_ref, yb_ref, g_ref, w_ref, wf_ref, bf_ref, _, _, xo_ref, q_ref, k_ref, v_ref, gate_ref, logf_ref = refs
        rows = x_ref.shape[0]
        x = x_ref[...] + (_load_row_tiled(ya_ref, rows) + _load_row_tiled(yb_ref, rows))
        xo_ref[...] = x
    else:
        x_ref, g_ref, w_ref, wf_ref, bf_ref, q_ref, k_ref, v_ref, gate_ref, logf_ref = refs
        x = x_ref[...]
    h = (_rms_scale(x) * g_ref[...]).astype(BF16)
    half = D_MODEL // 2
    q_scale = HEAD_DIM ** -0.5
    for c in range(7):
        z = jnp.dot(h, w_ref[:, c * half:(c + 1) * half], preferred_element_type=F32)
        dst = pl.ds((c % 2) * half, half)
        if c < 2:
            q_ref[:, dst] = (z * q_scale).astype(BF16)
        elif c < 4:
            k_ref[:, dst] = z
        elif c < 6:
            v_ref[:, dst] = z
        else:
            gate_ref[...] = z.astype(BF16)
    zf = jnp.dot(h, wf_ref[...], preferred_element_type=F32) + bf_ref[...]
    logf_ref[...] = _log_sigmoid(zf)


def _inproj(x, prev, g, w_main, w_f, b_f, tm, layer, depth):
    n = x.shape[0]
    nt = n // tm
    row = lambda c: pl.BlockSpec((tm, c), lambda i: (i, 0))
    full = lambda a: pl.BlockSpec(a.shape, lambda i: (0,) * a.ndim)
    layer_row = pl.BlockSpec((None, tm, D_MODEL), lambda i: (layer, i, 0))
    out_shape = [jax.ShapeDtypeStruct((n, D_MODEL), BF16),
                 jax.ShapeDtypeStruct((depth, n, D_MODEL), F32),
                 jax.ShapeDtypeStruct((depth, n, D_MODEL), F32),
                 jax.ShapeDtypeStruct((n, W_FOX), BF16),
                 jax.ShapeDtypeStruct((n, H_FOX), F32)]
    out_specs = [row(D_MODEL), layer_row, layer_row, row(W_FOX), row(H_FOX)]
    in_specs = [row(D_MODEL)]
    args = [x]
    tail_specs, tail_args, aliases = [], [], {}
    if prev is not None:
        y2, k_all, v_all = prev
        out_shape = [jax.ShapeDtypeStruct((n, D_MODEL), F32)] + out_shape
        out_specs = [row(D_MODEL)] + out_specs
        in_specs += [pl.BlockSpec((tm * ROW_TILES, LANES), lambda i: (i, 0)),
                     pl.BlockSpec((tm * ROW_TILES, LANES), lambda i: (i + nt, 0))]
        args += [y2, y2]
        tail_specs = [pl.BlockSpec(memory_space=pl.ANY)] * 2
        tail_args = [k_all, v_all]
        first_alias = len(args) + 4
        aliases = {first_alias: 2, first_alias + 1: 3}
    return pl.pallas_call(
        functools.partial(_inproj_body, prev is not None),
        grid=(nt,),
        in_specs=in_specs + [full(g), full(w_main), full(w_f), full(b_f)] + tail_specs,
        out_specs=out_specs,
        out_shape=out_shape,
        input_output_aliases=aliases,
        compiler_params=_cparams("parallel"),
        name="inproj",
    )(*args, g, w_main, w_f, b_f, *tail_args)


def _forget_cumsum_body(lf_ref, ccol_ref, crow_meta_ref, crow_ref):
    t = lf_ref.shape[1]
    n_blocks = (t - N_META) // ATTN_BLOCK
    meta = pl.ds(0, N_META)
    c = _split_dot_rhs(_tri(N_META, "prefix"), lf_ref[0, meta, :], 3)
    ccol_ref[0, meta, :] = c
    crow_meta_ref[0] = _transpose_small(c)
    carry0 = c[N_META - 1:N_META, :]
    prefix = _tri(ATTN_BLOCK, "prefix")

    def block(i, carry):
        r0 = pl.multiple_of(N_META + i * ATTN_BLOCK, 16)
        rows = pl.ds(r0, ATTN_BLOCK)
        c = _split_dot_rhs(prefix, lf_ref[0, rows, :], 3) + carry
        ccol_ref[0, rows, :] = c
        crow_ref[0, :, pl.ds(pl.multiple_of(i * ATTN_BLOCK, ATTN_BLOCK), ATTN_BLOCK)] = _transpose_small(c)
        return c[ATTN_BLOCK - 1:ATTN_BLOCK, :]

    lax.fori_loop(0, n_blocks, block, carry0)


def _forget_cumsum(logf):
    b, t, _ = logf.shape
    return pl.pallas_call(
        _forget_cumsum_body,
        grid=(b,),
        in_specs=[pl.BlockSpec((1, t, H_FOX), lambda i: (i, 0, 0))],
        out_specs=[pl.BlockSpec((1, t, H_FOX), lambda i: (i, 0, 0)),
                   pl.BlockSpec((1, H_FOX, N_META), lambda i: (i, 0, 0)),
                   pl.BlockSpec((1, H_FOX, t - N_META), lambda i: (i, 0, 0))],
        out_shape=[jax.ShapeDtypeStruct((b, t, H_FOX), F32),
                   jax.ShapeDtypeStruct((b, H_FOX, N_META), F32),
                   jax.ShapeDtypeStruct((b, H_FOX, t - N_META), F32)],
        compiler_params=_cparams("parallel"),
        name="forget_cumsum",
    )(logf)


def _sb_tile(q, k, v, tail, diag, off=0):
    tq, tk = q.shape[0], k.shape[0]
    z = _dot_nt(q, k)
    ls = _log_sigmoid_fast(z)
    l1m = ls - z
    if diag:
        row = lax.broadcasted_iota(jnp.int32, (tq, tk), 0)
        col = lax.broadcasted_iota(jnp.int32, (tq, tk), 1)
        valid = col < row + off
        l1m = jnp.where(valid, l1m, 0.0)
    suffix = _split_dot(l1m, _tri(tk, "suffix"), 2)
    w = jnp.exp(ls + suffix + tail)
    if diag:
        w = jnp.where(valid, w, 0.0)
    o = jnp.dot(w.astype(BF16), v, preferred_element_type=F32)
    tail = tail + suffix[:, 0:1] + l1m[:, 0:1]
    return o, tail


def _fx_tile(q, k, v, cq, ck, m, l, acc, diag, off=0):
    tq, tk = q.shape[0], k.shape[0]
    z = _dot_nt(q, k) + (cq - ck)
    if diag:
        row = lax.broadcasted_iota(jnp.int32, (tq, tk), 0)
        col = lax.broadcasted_iota(jnp.int32, (tq, tk), 1)
        z = jnp.where(col <= row + off, z, -jnp.inf)
    m_new = jnp.maximum(m, jnp.max(z, axis=-1, keepdims=True))
    alpha = jnp.exp(m - m_new)
    p = jnp.exp(z - m_new)
    l = alpha * l + jnp.sum(p, axis=-1, keepdims=True)
    acc = alpha * acc + jnp.dot(p.astype(BF16), v, preferred_element_type=F32)
    return m_new, l, acc


def _prompt_attn_body(qsb_ref, ksb_ref, vsb_ref, qfx_ref, kfx_ref, vfx_ref, ccol_ref, crm_ref, crr_ref,
                      osb_ref, ofx_ref, qs, ks, vs, acc, col, u_scr, z_scr, zm_scr):
    u_scr[...] = _tri(ATTN_BLOCK, "suffix")
    for g, (qr, kr, vr) in enumerate(((qsb_ref, ksb_ref, vsb_ref), (qfx_ref, kfx_ref, vfx_ref))):
        for hh in range(2):
            lanes = slice(hh * HEAD_DIM, (hh + 1) * HEAD_DIM)
            qs[2 * g + hh] = qr[0, :, lanes]
            ks[2 * g + hh] = kr[0, :, lanes].astype(BF16)
            vs[2 * g + hh] = vr[0, :, lanes].astype(BF16)
    t = qsb_ref.shape[1]
    n_blocks = (t - N_META) // Q_TILE
    pair = pl.program_id(1)
    meta = pl.ds(0, N_META)
    head_lane = lax.broadcasted_iota(jnp.int32, (1, H_FOX), 1)

    def query_bias(rows, hh):
        sel = head_lane == pair * 2 + hh
        return jnp.sum(jnp.where(sel, ccol_ref[0, rows, :], 0.0), axis=1, keepdims=True)

    def key_bias_meta(hh):
        return crm_ref[0, pl.ds(pair * 2 + hh, 1), :]

    def fx_init(tq):
        return (jnp.full((tq, 1), NEG_BIG, F32), jnp.zeros((tq, 1), F32), jnp.zeros((tq, HEAD_DIM), F32))

    for hh in range(2):
        lanes = slice(hh * HEAD_DIM, (hh + 1) * HEAD_DIM)
        o_meta, _ = _sb_tile(qs[hh, meta, :], ks[hh, meta, :], vs[hh, meta, :], jnp.zeros((N_META, 1), F32), True)
        osb_ref[0, meta, lanes] = o_meta
        g = 2 + hh
        _, l, a = _fx_tile(qs[g, meta, :], ks[g, meta, :], vs[g, meta, :], query_bias(meta, hh), key_bias_meta(hh),
                           *fx_init(N_META), True)
        ofx_ref[0, meta, lanes] = a / l

    def scores(rows, keys, z_ref):
        for g in range(4):
            z_ref[g] = _dot_nt(qs[g, rows, :], ks[g, keys, :])

    def q_rows(qi):
        return pl.ds(pl.multiple_of(N_META + qi * Q_TILE, 16), Q_TILE)

    def key_tile(kj):
        return pl.ds(pl.multiple_of(N_META + kj * ATTN_BLOCK, 16), ATTN_BLOCK)

    def diag_tile(qi):
        return (qi * Q_TILE) // ATTN_BLOCK

    def tiles(rows, keys, ck_of, diag, first, z_ref, prefetch, off=0):
        tq = Q_TILE
        tk = keys.size
        z = [z_ref[g] for g in range(4)]
        prefetch()
        if diag:
            row = lax.broadcasted_iota(jnp.int32, (tq, tk), 0)
            col_id = lax.broadcasted_iota(jnp.int32, (tq, tk), 1)
            strict = col_id < row + off
            incl = col_id <= row + off
        ls = [_log_sigmoid_fast(z[hh]) for hh in range(2)]
        l1m = [ls[hh] - z[hh] for hh in range(2)]
        if diag:
            l1m = [jnp.where(strict, x, 0.0) for x in l1m]
        u = u_scr[...] if tk == ATTN_BLOCK else _tri(tk, "suffix")
        sums = jnp.dot(jnp.concatenate([x.astype(BF16) for x in l1m], axis=0), u, preferred_element_type=F32)
        suffix = [sums[hh * tq:(hh + 1) * tq] for hh in range(2)]
        fx = []
        for hh in range(2):
            g = 2 + hh
            m_old, l_old, a_old = fx_init(tq) if first else (col[2 + hh], col[4 + hh], acc[g])
            zf = z[g] + (col[6 + hh] - ck_of(hh))
            if diag:
                zf = jnp.where(incl, zf, -jnp.inf)
            m_new = jnp.maximum(m_old, jnp.max(zf, axis=-1, keepdims=True))
            alpha = jnp.exp(m_old - m_new)
            p = jnp.exp(zf - m_new)
            col[2 + hh] = m_new
            col[4 + hh] = alpha * l_old + jnp.sum(p, axis=-1, keepdims=True)
            fx.append((alpha * a_old, p.astype(BF16)))
        for hh in range(2):
            g = 2 + hh
            acc[g] = fx[hh][0] + jnp.dot(fx[hh][1], vs[g, keys, :], preferred_element_type=F32)
        for hh in range(2):
            tail = jnp.zeros((tq, 1), F32) if first else col[hh]
            w = jnp.exp(ls[hh] + suffix[hh] + tail)
            if diag:
                w = jnp.where(strict, w, 0.0)
            o = jnp.dot(w.astype(BF16), vs[hh, keys, :], preferred_element_type=F32)
            acc[hh] = o if first else acc[hh] + o
            col[hh] = tail + suffix[hh][:, 0:1] + l1m[hh][:, 0:1]

    def block_scores(qi):
        scores(q_rows(qi), key_tile(diag_tile(qi)), z_scr)
        scores(q_rows(qi), meta, zm_scr)

    def q_block(qi, carry):
        rows = q_rows(qi)
        for hh in range(2):
            col[6 + hh] = query_bias(rows, hh)

        def key_bias(kj):
            c0 = pl.multiple_of(kj * ATTN_BLOCK, ATTN_BLOCK)
            return lambda hh: crr_ref[0, pl.ds(pair * 2 + hh, 1), pl.ds(c0, ATTN_BLOCK)]

        def next_scores(kj):
            return lambda: scores(rows, key_tile(jnp.maximum(kj - 1, 0)), z_scr)

        kd = diag_tile(qi)
        tiles(rows, key_tile(kd), key_bias(kd), True, True, z_scr, next_scores(kd),
              qi * Q_TILE - kd * ATTN_BLOCK)

        def k_block(step, c):
            kj = kd - 1 - step
            tiles(rows, key_tile(kj), key_bias(kj), False, False, z_scr, next_scores(kj))
            return c

        lax.fori_loop(0, kd, k_block, 0)
        tiles(rows, meta, key_bias_meta, False, False, zm_scr,
              lambda: block_scores(jnp.minimum(qi + 1, n_blocks - 1)))
        for hh in range(2):
            lanes = slice(hh * HEAD_DIM, (hh + 1) * HEAD_DIM)
            osb_ref[0, rows, lanes] = acc[hh]
            ofx_ref[0, rows, lanes] = acc[2 + hh] / col[4 + hh]
        return carry

    block_scores(0)
    lax.fori_loop(0, n_blocks, q_block, 0)


def _prompt_attn(layer, q, k, v, ccol, crow_meta, crow):
    b, t, _ = q.shape
    n_pairs = H_SB // 2
    sb_col = pl.BlockSpec((1, t, LANES), lambda i, j: (i, 0, j))
    fx_col = pl.BlockSpec((1, t, LANES), lambda i, j: (i, 0, j + n_pairs))
    sb_kv = pl.BlockSpec((None, 1, t, LANES), lambda i, j: (layer, i, 0, j))
    fx_kv = pl.BlockSpec((None, 1, t, LANES), lambda i, j: (layer, i, 0, j + n_pairs))
    whole = lambda a: pl.BlockSpec((1,) + a.shape[1:], lambda i, j: (i, 0, 0))
    return pl.pallas_call(
        _prompt_attn_body,
        grid=(b, n_pairs),
        in_specs=[sb_col, sb_kv, sb_kv, fx_col, fx_kv, fx_kv, whole(ccol), whole(crow_meta), whole(crow)],
        out_specs=[sb_col, sb_col],
        out_shape=[jax.ShapeDtypeStruct((b, t, W_SB), F32), jax.ShapeDtypeStruct((b, t, W_FOX), F32)],
        scratch_shapes=[pltpu.VMEM((4, t, HEAD_DIM), BF16)] * 3
                       + [pltpu.VMEM((4, Q_TILE, HEAD_DIM), F32), pltpu.VMEM((8, Q_TILE, 1), F32),
                          pltpu.VMEM((ATTN_BLOCK, ATTN_BLOCK), BF16),
                          pltpu.VMEM((4, Q_TILE, ATTN_BLOCK), F32), pltpu.VMEM((4, Q_TILE, N_META), F32)],
        compiler_params=_cparams("parallel", "parallel"),
        name="prompt_attn",
    )(q, k, v, q, k, v, ccol, crow_meta, crow)


def _sample_body(n_new, pt_ref, q_ref, kn_ref, vn_ref, lfn_ref, *refs):
    npg = PAGES_PER_STEP
    k_refs = refs[0:npg]
    v_refs = refs[npg:2 * npg]
    lf_refs = refs[2 * npg:3 * npg]
    o_ref = refs[3 * npg]
    q_scr, acc, m_scr, l_scr, tail_scr, rc_scr, cq_scr, u_scr = refs[3 * npg + 1:]
    sb_rows = H_SB * n_new
    step = pl.program_id(1)
    tk = npg * PAGE_SIZE
    head_rows = lambda x, h: x[h * n_new:(h + 1) * n_new]
    head_lanes = lambda h: slice(h * HEAD_DIM, (h + 1) * HEAD_DIM)

    @pl.when(step == 0)
    def _():
        for h in range(N_HEADS):
            q_scr[h] = q_ref[:, head_lanes(h)].astype(F32)
        s_new = jnp.concatenate([_dot_nt(q_scr[h], kn_ref[:, head_lanes(h)]) for h in range(N_HEADS)], axis=0)
        qi = lax.broadcasted_iota(jnp.int32, (N_HEADS * n_new, n_new), 0) % n_new
        ki = lax.broadcasted_iota(jnp.int32, (N_HEADS * n_new, n_new), 1)
        z = s_new[:sb_rows]
        strict = (ki < qi)[:sb_rows]
        ls = _log_sigmoid_fast(z)
        l1m = jnp.where(strict, ls - z, 0.0)
        suffix = _split_dot(l1m, _tri(n_new, "suffix"), 3)
        w_sb = jnp.where(strict, jnp.exp(ls + suffix), 0.0)
        tail_scr[...] = jnp.sum(l1m, axis=-1, keepdims=True)
        cn = _split_dot_rhs(_tri(n_new, "prefix"), lfn_ref[...], 3)
        cn_t = _transpose_small(cn)
        cq = jnp.concatenate([cn[:, h:h + 1] for h in range(H_FOX)], axis=0)
        ck = jnp.concatenate([jnp.broadcast_to(cn_t[h:h + 1, :], (n_new, n_new)) for h in range(H_FOX)], axis=0)
        cq_scr[...] = cq
        zf = jnp.where((ki <= qi)[sb_rows:], s_new[sb_rows:] + (cq - ck), -jnp.inf)
        m0 = jnp.max(zf, axis=-1, keepdims=True)
        p = jnp.exp(zf - m0)
        m_scr[...] = m0
        l_scr[...] = jnp.sum(p, axis=-1, keepdims=True)
        rc_scr[...] = jnp.zeros_like(rc_scr)
        u_scr[...] = _tri(tk, "suffix")
        pw = jnp.concatenate([w_sb, p], axis=0)
        acc[...] = jnp.concatenate(
            [jnp.dot(head_rows(pw, h), vn_ref[:, head_lanes(h)], preferred_element_type=F32)
             for h in range(N_HEADS)], axis=0)

    def head_t(refs_, h):
        return jnp.concatenate([r[h].astype(BF16) for r in refs_], axis=1)

    s = jnp.concatenate([jnp.dot(q_scr[h].astype(BF16), head_t(k_refs, h), preferred_element_type=F32)
                         for h in range(N_HEADS)], axis=0)
    u = u_scr[...]
    z = s[:sb_rows]
    ls = _log_sigmoid_fast(z)
    l1m = ls - z
    suffix = _split_dot(l1m, u, 2)
    tail = tail_scr[...]
    w_sb = jnp.exp(ls + suffix + tail)
    tail_scr[...] = tail + suffix[:, 0:1] + l1m[:, 0:1]
    plf_t = jnp.concatenate([lf_refs[j][...] for j in range(npg)], axis=1)
    rc_prev = rc_scr[...]
    rc = _split_dot(plf_t, u, 3) + rc_prev
    rc_scr[...] = rc_prev + jnp.sum(plf_t, axis=-1, keepdims=True)
    bias = jnp.concatenate([jnp.broadcast_to(rc[h:h + 1, :], (n_new, tk)) for h in range(H_FOX)], axis=0)
    zf = s[sb_rows:] + (cq_scr[...] + bias)
    m_prev = m_scr[...]
    m_new = jnp.maximum(m_prev, jnp.max(zf, axis=-1, keepdims=True))
    alpha = jnp.exp(m_prev - m_new)
    p = jnp.exp(zf - m_new)
    m_scr[...] = m_new
    l_scr[...] = alpha * l_scr[...] + jnp.sum(p, axis=-1, keepdims=True)
    pw = jnp.concatenate([w_sb, p], axis=0)
    scale = jnp.concatenate([jnp.ones((sb_rows, 1), F32), alpha], axis=0)
    pv = jnp.concatenate(
        [_dot_nt(head_rows(pw, h).astype(BF16), head_t(v_refs, h)) for h in range(N_HEADS)], axis=0)
    acc[...] = acc[...] * scale + pv

    @pl.when(step == pl.num_programs(1) - 1)
    def _():
        norm = jnp.concatenate([jnp.ones((sb_rows, 1), F32), 1.0 / l_scr[...]], axis=0)
        out = acc[...] * norm
        for h in range(N_HEADS):
            o_ref[:, head_lanes(h)] = head_rows(out, h)


def _sample_attn(layer, page_table, q, k_new, v_new, logf_new, cache_k, cache_v, cache_logf, n_new):
    db, n_pages = page_table.shape
    npg = PAGES_PER_STEP
    n_steps = n_pages // npg
    rows = N_HEADS * n_new
    fx_rows = H_FOX * n_new

    def page_of(b, s, pt, j):
        return pt[b, n_pages - (s + 1) * npg + j]

    per_row = lambda c: pl.BlockSpec((n_new, c), lambda b, s, pt: (b, 0))
    new_kv = pl.BlockSpec((None, n_new, D_MODEL), lambda b, s, pt: (layer, b, 0))
    kv_specs =[pl.BlockSpec((None, None, N_HEADS, HEAD_DIM, PAGE_SIZE),
                             functools.partial(lambda b, s, pt, j: (layer, page_of(b, s, pt, j), 0, 0, 0), j=j))
                for j in range(npg)]
    lf_specs = [pl.BlockSpec((None, None, H_FOX, PAGE_SIZE),
                             functools.partial(lambda b, s, pt, j: (layer, page_of(b, s, pt, j), 0, 0), j=j))
                for j in range(npg)]
    grid_spec = pltpu.PrefetchScalarGridSpec(
        num_scalar_prefetch=1,
        grid=(db, n_steps),
        in_specs=[per_row(D_MODEL), new_kv, new_kv, per_row(H_FOX)] + kv_specs + kv_specs + lf_specs,
        out_specs=per_row(D_MODEL),
        scratch_shapes=[pltpu.VMEM((N_HEADS, n_new, HEAD_DIM), F32),
                        pltpu.VMEM((rows, HEAD_DIM), F32),
                        pltpu.VMEM((fx_rows, 1), F32),
                        pltpu.VMEM((fx_rows, 1), F32),
                        pltpu.VMEM((rows - fx_rows, 1), F32),
                        pltpu.VMEM((H_FOX, 1), F32),
                        pltpu.VMEM((fx_rows, 1), F32),
                        pltpu.VMEM((npg * PAGE_SIZE, npg * PAGE_SIZE), BF16)],
    )
    return pl.pallas_call(
        functools.partial(_sample_body, n_new),
        grid_spec=grid_spec,
        out_shape=jax.ShapeDtypeStruct((db * n_new, D_MODEL), F32),
        compiler_params=_cparams("parallel", "arbitrary"),
        name="sample_attn",
    )(page_table, q, k_new, v_new, logf_new, *([cache_k] * npg), *([cache_v] * npg), *([cache_logf] * npg))


def _route(logits):
    rows = logits.shape[0]
    g = logits[:, :N_GROUPS]
    g_lane = lax.broadcasted_iota(jnp.int32, (rows, N_GROUPS), 1).astype(F32)
    g_max = jnp.max(g, axis=1, keepdims=True)
    grp = jnp.min(jnp.where(g == g_max, g_lane, float(N_GROUPS)), axis=1, keepdims=True)
    p_grp = 1.0 / jnp.sum(jnp.exp(g - g_max), axis=1, keepdims=True)
    e = logits[:, N_GROUPS:]
    lane = lax.broadcasted_iota(jnp.int32, (rows, N_EXPERTS), 1)
    e_lane = lane.astype(F32)
    in_group = (lane // EXPERTS_PER_GROUP).astype(F32) == grp
    m1 = jnp.where(in_group, e, -jnp.inf)
    v1 = jnp.max(m1, axis=1, keepdims=True)
    i1 = jnp.min(jnp.where(m1 == v1, e_lane, float(N_EXPERTS)), axis=1, keepdims=True)
    m2 = jnp.where(e_lane == i1, -jnp.inf, m1)
    v2 = jnp.max(m2, axis=1, keepdims=True)
    i2 = jnp.min(jnp.where(m2 == v2, e_lane, float(N_EXPERTS)), axis=1, keepdims=True)
    t = jnp.exp(v2 - v1)
    g1 = p_grp / (1.0 + t)
    return i1, i2, g1, g1 * t


def _merge_body(x_ref, o_ref, gate_ref, gsb_ref, gfx_ref, wo_ref, gffn_ref, wr_ref, br_ref,
                xmid_ref, h_ref, eid_ref, egate_ref):
    a_sb = (_rms_scale(o_ref[:, :W_SB]) * gsb_ref[...]).astype(BF16)
    a_fx = (_rms_scale(o_ref[:, W_SB:]) * gfx_ref[...] * jax.nn.sigmoid(gate_ref[...].astype(F32))).astype(BF16)
    y = (jnp.dot(a_sb, wo_ref[:W_SB, :], preferred_element_type=F32)
         + jnp.dot(a_fx, wo_ref[W_SB:, :], preferred_element_type=F32))
    x = x_ref[...] + y
    xmid_ref[...] = x
    h = _rms_scale(x) * gffn_ref[...]
    _store_row_tiled(h_ref, h)
    logits = jnp.dot(h, wr_ref[...], preferred_element_type=F32, precision=lax.Precision.HIGHEST) + br_ref[...]
    i1, i2, g1, g2 = _route(logits)
    eid_ref[:, 0:1] = i1.astype(jnp.int32)
    eid_ref[:, 1:2] = i2.astype(jnp.int32)
    egate_ref[:, 0:1] = g1
    egate_ref[:, 1:2] = g2


def _merge(x, o_parts, gate, g_sb, g_fx, w_out, g_ffn, w_route, b_route, tm):
    n = x.shape[0]
    row = lambda c: pl.BlockSpec((tm, c), lambda i: (i, 0))
    full = lambda a: pl.BlockSpec(a.shape, lambda i: (0,) * a.ndim)
    body = _merge_body
    if len(o_parts) == 2:
        def body(x_ref, osb_ref, ofx_ref, *rest):
            return _merge_body(x_ref, _Halves(osb_ref, ofx_ref), *rest)
    return pl.pallas_call(
        body,
        grid=(n // tm,),
        in_specs=[row(D_MODEL)] + [row(o.shape[1]) for o in o_parts]
                 + [row(W_FOX), full(g_sb), full(g_fx), full(w_out), full(g_ffn), full(w_route), full(b_route)],
        out_specs=[row(D_MODEL), pl.BlockSpec((tm * ROW_TILES, LANES), lambda i: (i, 0)), row(2), row(2)],
        out_shape=[jax.ShapeDtypeStruct((n, D_MODEL), F32),
                   jax.ShapeDtypeStruct((n * ROW_TILES, LANES), F32),
                   jax.ShapeDtypeStruct((n, 2), jnp.int32),
                   jax.ShapeDtypeStruct((n, 2), F32)],
        compiler_params=_cparams("parallel"),
        name="merge",
    )(x, *o_parts, gate, g_sb, g_fx, w_out, g_ffn, w_route, b_route)


class _Halves:
    def __init__(self, lo, hi):
        self.lo, self.hi = lo, hi

    def __getitem__(self, idx):
        rows, cols = idx
        if cols == slice(None, W_SB):
            return self.lo[rows, :]
        assert cols == slice(W_SB, None)
        return self.hi[rows, :]


def _expert_body(n_tok, be_ref, nv_ref, dst_ref, h_hbm, gate_ref, w1_ref, w3_ref, w2_ref, y_hbm,
                 xbuf, ybuf, gsem, ssem):
    i = pl.program_id(0)
    nb = pl.num_programs(0)
    slot = i % 2
    tile = lambda r: pl.ds(pl.multiple_of(r * ROW_TILES, ROW_TILES), ROW_TILES)

    def gather(b, sl):
        def one(r, c):
            d = dst_ref[b * MOE_BLOCK + r]
            tok = jnp.where(d >= n_tok, d - n_tok, d)
            pltpu.make_async_copy(h_hbm.at[tile(tok), :], xbuf.at[sl, tile(r), :], gsem.at[sl]).start()
            return c
        lax.fori_loop(0, nv_ref[b], one, 0)

    def gather_wait(b, sl):
        n = nv_ref[b] * ROW_TILES

        @pl.when(n > 0)
        def _():
            pltpu.make_async_copy(h_hbm.at[pl.ds(0, n), :], xbuf.at[sl, pl.ds(0, n), :], gsem.at[sl]).wait()

    def scatter(b, sl):
        def one(r, c):
            d = dst_ref[b * MOE_BLOCK + r]
            pltpu.make_async_copy(ybuf.at[sl, tile(r), :], y_hbm.at[tile(d), :], ssem.at[sl]).start()
            return c
        lax.fori_loop(0, nv_ref[b], one, 0)

    def scatter_wait(b, sl):
        n = nv_ref[b] * ROW_TILES

        @pl.when(n > 0)
        def _():
            pltpu.make_async_copy(ybuf.at[sl, pl.ds(0, n), :], y_hbm.at[pl.ds(0, n), :], ssem.at[sl]).wait()

    @pl.when(i == 0)
    def _():
        xbuf[...] = jnp.zeros_like(xbuf)
        gather(0, 0)

    @pl.when(i + 1 < nb)
    def _():
        gather(i + 1, 1 - slot)

    gather_wait(i, slot)

    @pl.when(i >= 2)
    def _():
        scatter_wait(i - 2, slot)

    @pl.when(nv_ref[i] > 0)
    def _():
        x = _load_row_tiled(xbuf.at[slot], MOE_BLOCK).astype(BF16)
        a = jnp.dot(x, w1_ref[...].astype(BF16), preferred_element_type=F32)
        b = jnp.dot(x, w3_ref[...].astype(BF16), preferred_element_type=F32)
        hmid = (a * jax.nn.sigmoid(a) * b).astype(BF16)
        y = jnp.dot(hmid, w2_ref[...].astype(BF16), preferred_element_type=F32) * gate_ref[...]
        _store_row_tiled(ybuf, y, (slot,))
        scatter(i, slot)

    @pl.when(i == nb - 1)
    def _():
        @pl.when(i >= 1)
        def _():
            scatter_wait(i - 1, 1 - slot)
        scatter_wait(i, slot)


def _experts(n_tok, block_exp, n_valid, dst_row, h_rt, slot_gate, layer, w1, w3, w2):
    n_blocks = block_exp.shape[0]
    buf = pltpu.VMEM((2, MOE_BLOCK * ROW_TILES, LANES), F32)
    expert = lambda i, be, nv, dst: (layer, be[i], 0, 0)
    grid_spec = pltpu.PrefetchScalarGridSpec(
        num_scalar_prefetch=3,
        grid=(n_blocks,),
        in_specs=[pl.BlockSpec(memory_space=pl.ANY),
                  pl.BlockSpec((MOE_BLOCK, 1), lambda i, be, nv, dst: (i, 0)),
                  pl.BlockSpec((None, None, D_MODEL, D_EXPERT), expert),
                  pl.BlockSpec((None, None, D_MODEL, D_EXPERT), expert),
                  pl.BlockSpec((None, None, D_EXPERT, D_MODEL), expert)],
        out_specs=pl.BlockSpec(memory_space=pl.ANY),
        scratch_shapes=[buf, buf, pltpu.SemaphoreType.DMA((2,)), pltpu.SemaphoreType.DMA((2,))],
    )
    return pl.pallas_call(
        functools.partial(_expert_body, n_tok),
        grid_spec=grid_spec,
        out_shape=jax.ShapeDtypeStruct((2 * n_tok * ROW_TILES, LANES), F32),
        compiler_params=_cparams("arbitrary"),
        name="experts",
    )(block_exp, n_valid, dst_row, h_rt, slot_gate, w1, w3, w2)


def _moe(h_rt, eid, gate, layer, w1, w3, w2):
    n_tok = eid.shape[0]
    n_asg = n_tok * 2
    flat_e = eid.reshape(n_asg)
    order = jnp.argsort(flat_e).astype(jnp.int32)
    counts = jnp.sum((flat_e[:, None] == jnp.arange(N_EXPERTS, dtype=jnp.int32)[None, :]).astype(jnp.int32), axis=0)
    padded = (counts + MOE_BLOCK - 1) // MOE_BLOCK * MOE_BLOCK
    ends_p = jnp.cumsum(padded)
    run_start = ends_p - padded
    sorted_start = jnp.cumsum(counts) - counts
    n_blocks = -(-n_asg // MOE_BLOCK) + N_EXPERTS
    block_start = jnp.arange(n_blocks, dtype=jnp.int32) * MOE_BLOCK
    block_exp = jnp.minimum(jnp.sum((ends_p[None, :] <= block_start[:, None]).astype(jnp.int32), axis=1),
                            N_EXPERTS - 1).astype(jnp.int32)
    block_off = block_start - run_start[block_exp]
    n_valid = jnp.clip(counts[block_exp] - block_off, 0, MOE_BLOCK).astype(jnp.int32)
    in_block = jnp.arange(MOE_BLOCK, dtype=jnp.int32)[None, :]
    valid = in_block < n_valid[:, None]
    pos = jnp.clip((sorted_start[block_exp] + block_off)[:, None] + in_block, 0, n_asg - 1)
    asg = order[pos.reshape(-1)]
    dst_row = jnp.where(valid.reshape(-1), (asg % 2) * n_tok + asg // 2, 0).astype(jnp.int32)
    slot_gate = jnp.where(valid.reshape(-1), gate.reshape(n_asg)[asg], 0.0)
    return _experts(n_tok, block_exp, n_valid, dst_row, h_rt, slot_gate[:, None], layer, w1, w3, w2)


def _final_body(x_ref, ya_ref, yb_ref, g_ref, o_ref):
    rows = x_ref.shape[0]
    x = x_ref[...] + (_load_row_tiled(ya_ref, rows) + _load_row_tiled(yb_ref, rows))
    o_ref[...] = _rms_scale(x) * g_ref[...]


def _final_norm(x, y2, g, tm):
    n = x.shape[0]
    nt = n // tm
    row = pl.BlockSpec((tm, D_MODEL), lambda i: (i, 0))
    return pl.pallas_call(
        _final_body,
        grid=(nt,),
        in_specs=[row,
                  pl.BlockSpec((tm * ROW_TILES, LANES), lambda i: (i, 0)),
                  pl.BlockSpec((tm * ROW_TILES, LANES), lambda i: (i + nt, 0)),
                  pl.BlockSpec(g.shape, lambda i: (0, 0))],
        out_specs=row,
        out_shape=jax.ShapeDtypeStruct((n, D_MODEL), F32),
        compiler_params=_cparams("parallel"),
        name="final_norm",
    )(x, y2, y2, g)


def _row_tile(n):
    for tm in (768, 512, 384, 256, 128, 64, 32, 16, 8):
        if n % tm == 0:
            return tm
    raise ValueError(f"row count {n} is not a multiple of 8")


def kernel(x_prompt, x_sample, cache_k, cache_v, cache_logf, page_table, meta_tokens, norm_mix_g, w_in, b_forget,
           norm_sb_g, norm_fox_g, w_out, norm_ffn_g, w_group, b_group, w_router, b_router, w_exp_gate, w_exp_up,
           w_exp_down, norm_final_g):
    bsz, seq, _ = x_prompt.shape
    db, n_new, _ = x_sample.shape
    depth = w_in.shape[0]
    t = seq + N_META
    n_p = bsz * t
    n_s = db * n_new
    assert (seq % ATTN_BLOCK) == 0 and page_table.shape[1] % PAGES_PER_STEP == 0

    meta = jnp.broadcast_to(meta_tokens[None].astype(x_prompt.dtype), (bsz, N_META, D_MODEL))
    xp = jnp.concatenate([meta, x_prompt], axis=1).reshape(n_p, D_MODEL)
    xs = x_sample.reshape(n_s, D_MODEL)
    ck_t = jnp.transpose(cache_k, (0, 1, 3, 4, 2))
    cv_t = jnp.transpose(cache_v, (0, 1, 3, 4, 2))
    clf_t = jnp.transpose(cache_logf, (0, 1, 3, 2))
    tm_p, tm_s = _row_tile(n_p), _row_tile(n_s)
    n_main = 3 * D_MODEL + W_FOX

    streams = {"p": (xp, None, None, None), "s": (xs, None, None, None)}
    logfs = {name: [] for name in streams}
    for l in range(depth):
        w_main = w_in[l, :, :n_main].astype(BF16)
        w_f = w_in[l, :, n_main:].astype(BF16)
        b_f = b_forget[l][None, :]
        g_mix = norm_mix_g[l][None, :]
        w_o = w_out[l].astype(BF16)
        w_route = jnp.concatenate([w_group[l], w_router[l]], axis=1)
        b_route = jnp.concatenate([b_group[l], b_router[l]])[None, :]
        new_streams = {}
        for name, tm in (("p", tm_p), ("s", tm_s)):
            x, y2, k_all, v_all = streams[name]
            prev = None if y2 is None else (y2, k_all, v_all)
            r = _inproj(x, prev, g_mix, w_main, w_f, b_f, tm, l, depth)
            if prev is not None:
                x, r = r[0], r[1:]
            q, k_all, v_all, gate, logf = r
            logfs[name].append(logf)
            if name == "p":
                k4, v4 = (a.reshape(depth, bsz, t, D_MODEL) for a in (k_all, v_all))
                ccol, crow_meta, crow = _forget_cumsum(logf.reshape(bsz, t, H_FOX))
                o_sb, o_fx = _prompt_attn(l, q.reshape(bsz, t, D_MODEL), k4, v4, ccol, crow_meta, crow)
                o_parts = (o_sb.reshape(n_p, W_SB), o_fx.reshape(n_p, W_FOX))
            else:
                o_parts = (_sample_attn(l, page_table, q, k_all, v_all, logf, ck_t, cv_t, clf_t, n_new),)
            x_mid, h_rt, eid, egate = _merge(x, o_parts, gate, norm_sb_g[l][None, :], norm_fox_g[l][None, :], w_o,
                                             norm_ffn_g[l][None, :], w_route, b_route, tm)
            y2 = _moe(h_rt, eid, egate, l, w_exp_gate, w_exp_up, w_exp_down)
            new_streams[name] = (x_mid, y2, k_all, v_all)
        streams = new_streams

    g_fin = norm_final_g[None, :]
    y_prompt = _final_norm(*streams["p"][:2], g_fin, tm_p).reshape(bsz, t, D_MODEL)[:, N_META:]
    y_sample = _final_norm(*streams["s"][:2], g_fin, tm_s).reshape(db, n_new, D_MODEL)
    heads_s = (depth, db, n_new, N_HEADS, HEAD_DIM)

    def heads_p(a):
        a = jnp.transpose(a.reshape(depth, bsz, t, D_MODEL), (0, 1, 3, 2))
        return jnp.transpose(a.reshape(depth, bsz, N_HEADS, HEAD_DIM, t), (0, 1, 4, 2, 3))

    return (y_prompt, y_sample,
            heads_p(streams["p"][2]),
            heads_p(streams["p"][3]),
            jnp.stack([a.reshape(bsz, t, H_FOX) for a in logfs["p"]]),
            streams["s"][2].reshape(heads_s),
            streams["s"][3].reshape(heads_s),
            jnp.stack([a.reshape(db, n_new, H_FOX) for a in logfs["s"]]))
```
